```python
import jax, jax.numpy as jnp
from jax import lax
import numpy as np

D_MODEL = 1024
BATCH = 4
SEQ = 8192
DEPTH = 1

CHUNK = 64
GMLP_BLOCK = 128
GMLP_GROUP_DIM = 128
GMLP_WIDTH = D_MODEL
GMLP_GROUPS = GMLP_WIDTH // GMLP_GROUP_DIM
S5_GROUP_DIM = 16
S5_WIDTH = D_MODEL // 2
S5_GROUPS = S5_WIDTH // S5_GROUP_DIM
S5_STATE = 64
N_BRANCHES = 2
IN_WIDTH = 2 * GMLP_WIDTH + S5_WIDTH + N_BRANCHES * D_MODEL
FFN_HIDDEN = -(-8 * D_MODEL // 768) * 256
DT_MIN = 1e-3
DT_MAX = 1e-1
EPS = 1e-6

kernel_name = "hybrid_gmlp_s5_gated_streaming_block"


def rms_norm(x, g):
    xf = x.astype(jnp.float32)
    y = xf * lax.rsqrt(jnp.mean(xf * xf, axis=-1, keepdims=True) + EPS)
    return (y * g.astype(jnp.float32)).astype(x.dtype)


def layer_norm(x, g, b):
    xf = x.astype(jnp.float32)
    mu = jnp.mean(xf, axis=-1, keepdims=True)
    xc = xf - mu
    y = xc * lax.rsqrt(jnp.mean(xc * xc, axis=-1, keepdims=True) + EPS)
    return (y * g.astype(jnp.float32) + b.astype(jnp.float32)).astype(x.dtype)


def gmlp_mixer(u, v, ln_g, ln_b, ws, bs):
    bsz, length, _ = u.shape
    u = jax.nn.gelu(u)
    v = layer_norm(jax.nn.gelu(v), ln_g, ln_b)
    cidx = jnp.arange(GMLP_BLOCK) // CHUNK
    mask = cidx[None, :] <= cidx[:, None]
    ws_m = jnp.where(mask[None], ws, jnp.zeros_like(ws))
    vb = v.reshape(bsz, length // GMLP_BLOCK, GMLP_BLOCK, GMLP_GROUPS, GMLP_GROUP_DIM)
    mixed = jnp.einsum('gij,bnjgc->bnigc', ws_m, vb) + bs.T[:, :, None]
    return u * mixed.reshape(bsz, length, GMLP_WIDTH)


def _ssm_combine(e1, e2):
    a1r, a1i, b1r, b1i = e1
    a2r, a2i, b2r, b2i = e2
    return (a2r * a1r - a2i * a1i,
            a2r * a1i + a2i * a1r,
            a2r * b1r - a2i * b1i + b2r,
            a2r * b1i + a2i * b1r + b2i)


def s5_mixer(xb, lam_re, lam_im, log_dt, b_re, b_im, c_re, c_im, d, w_glu, b_glu):
    dtype = xb.dtype
    f32 = jnp.float32
    bsz, length, _ = xb.shape
    dt = jnp.exp(log_dt.astype(f32))[:, None]
    lr = lam_re.astype(f32)
    li = lam_im.astype(f32)
    mag = jnp.exp(lr * dt)
    ab_re = mag * jnp.cos(li * dt)
    ab_im = mag * jnp.sin(li * dt)
    den = lr * lr + li * li
    nr = ab_re - 1.0
    coef_re = (nr * lr + ab_im * li) / den
    coef_im = (ab_im * lr - nr * li) / den
    br = b_re.astype(f32)
    bi = b_im.astype(f32)
    bb_re = coef_re[..., None] * br - coef_im[..., None] * bi
    bb_im = coef_re[..., None] * bi + coef_im[..., None] * br
    cr = c_re.astype(f32)
    ci = c_im.astype(f32)
    df = d.astype(f32)

    u = xb.astype(f32).reshape(bsz, length // CHUNK, CHUNK, S5_GROUPS, S5_GROUP_DIM)
    u = u.transpose(1, 0, 2, 3, 4)

    def step(carry, u_c):
        h_re, h_im = carry
        bu_re = jnp.einsum('bcgh,gph->bcgp', u_c, bb_re)
        bu_im = jnp.einsum('bcgh,gph->bcgp', u_c, bb_im)
        a_re = jnp.broadcast_to(ab_re, bu_re.shape)
        a_im = jnp.broadcast_to(ab_im, bu_im.shape)
        pr, pim, xr, xi = lax.associative_scan(_ssm_combine, (a_re, a_im, bu_re, bu_im), axis=1)
        xr = xr + pr * h_re[:, None] - pim * h_im[:, None]
        xi = xi + pr * h_im[:, None] + pim * h_re[:, None]
        y = (jnp.einsum('bcgp,ghp->bcgh', xr, cr)
             - jnp.einsum('bcgp,ghp->bcgh', xi, ci)
             + df * u_c)
        return (xr[:, -1], xi[:, -1]), y

    h0 = jnp.zeros((bsz, S5_GROUPS, S5_STATE), f32)
    _, ys = lax.scan(step, (h0, h0), u)
    y = ys.transpose(1, 0, 2, 3, 4).reshape(bsz, length, S5_WIDTH)
    y = jax.nn.gelu(y)
    y = y * jax.nn.sigmoid(y @ w_glu.astype(f32) + b_glu.astype(f32))
    return y.astype(dtype)


def hybrid_layer(x, norm1_g, w_in, gmlp_ln_g, gmlp_ln_b, gmlp_ws, gmlp_bs,
                 s5_lambda_re, s5_lambda_im, s5_log_dt, s5_b_re, s5_b_im, s5_c_re, s5_c_im,
                 s5_d, s5_w_glu, s5_b_glu, w_branch_a, w_branch_b, w_out,
                 norm2_g, w_ffn_gate, w_ffn_up, w_ffn_down):
    h = rms_norm(x, norm1_g)
    proj = h @ w_in
    s1 = GMLP_WIDTH
    s2 = 2 * GMLP_WIDTH
    s3 = s2 + S5_WIDTH
    s4 = s3 + D_MODEL
    u_a = proj[..., :s1]
    v_a = proj[..., s1:s2]
    x_b = proj[..., s2:s3]
    g_a = proj[..., s3:s4]
    g_b = proj[..., s4:]
    y_a = gmlp_mixer(u_a, v_a, gmlp_ln_g, gmlp_ln_b, gmlp_ws, gmlp_bs)
    y_b = s5_mixer(x_b, s5_lambda_re, s5_lambda_im, s5_log_dt, s5_b_re, s5_b_im,
                   s5_c_re, s5_c_im, s5_d, s5_w_glu, s5_b_glu)
    merged = jax.nn.sigmoid(g_a) * (y_a @ w_branch_a) + jax.nn.sigmoid(g_b) * (y_b @ w_branch_b)
    x = x + merged @ w_out
    h2 = rms_norm(x, norm2_g)
    x = x + (jax.nn.silu(h2 @ w_ffn_gate) * (h2 @ w_ffn_up)) @ w_ffn_down
    return x


def setup_inputs(seed: int = 0) -> dict:
    key = jax.random.key(seed)
    ks = jax.random.split(key, 32)
    f32 = jnp.float32
    L = DEPTH
    G, P, H = S5_GROUPS, S5_STATE, S5_GROUP_DIM
    nrm = lambda k, shape, s: jax.random.normal(k, shape, f32) * s
    log_dt = jax.random.uniform(ks[9], (L, G), f32, np.log(DT_MIN), np.log(DT_MAX))
    return {
        "x": jax.random.normal(ks[0], (BATCH, SEQ, D_MODEL), f32),
        "norm1_g": 1.0 + nrm(ks[1], (L, D_MODEL), 0.01),
        "w_in": nrm(ks[2], (L, D_MODEL, IN_WIDTH), D_MODEL ** -0.5),
        "gmlp_ln_g": 1.0 + nrm(ks[3], (L, GMLP_WIDTH), 0.01),
        "gmlp_ln_b": nrm(ks[4], (L, GMLP_WIDTH), 0.01),
        "gmlp_ws": nrm(ks[5], (L, GMLP_GROUPS, GMLP_BLOCK, GMLP_BLOCK), GMLP_BLOCK ** -0.5),
        "gmlp_bs": 1.0 + nrm(ks[6], (L, GMLP_GROUPS, GMLP_BLOCK), 0.01),
        "s5_lambda_re": -0.5 + nrm(ks[7], (L, G, P), 0.01),
        "s5_lambda_im": jnp.broadcast_to(np.pi * jnp.arange(P, dtype=f32), (L, G, P)) + nrm(ks[8], (L, G, P), 0.01),
        "s5_log_dt": log_dt,
        "s5_b_re": nrm(ks[10], (L, G, P, H), (2.0 * H) ** -0.5),
        "s5_b_im": nrm(ks[11], (L, G, P, H), (2.0 * H) ** -0.5),
        "s5_c_re": nrm(ks[12], (L, G, H, P), (2.0 * P) ** -0.5),
        "s5_c_im": nrm(ks[13], (L, G, H, P), (2.0 * P) ** -0.5),
        "s5_d": nrm(ks[14], (L, G, H), 1.0),
        "s5_w_glu": nrm(ks[15], (L, S5_WIDTH, S5_WIDTH), S5_WIDTH ** -0.5),
        "s5_b_glu": nrm(ks[16], (L, S5_WIDTH), 0.01),
        "w_branch_a": nrm(ks[17], (L, GMLP_WIDTH, D_MODEL), GMLP_WIDTH ** -0.5),
        "w_branch_b": nrm(ks[18], (L, S5_WIDTH, D_MODEL), S5_WIDTH ** -0.5),
        "w_out": nrm(ks[19], (L, D_MODEL, D_MODEL), D_MODEL ** -0.5),
        "norm2_g": 1.0 + nrm(ks[20], (L, D_MODEL), 0.01),
        "w_ffn_gate": nrm(ks[21], (L, D_MODEL, FFN_HIDDEN), D_MODEL ** -0.5),
        "w_ffn_up": nrm(ks[22], (L, D_MODEL, FFN_HIDDEN), D_MODEL ** -0.5),
        "w_ffn_down": nrm(ks[23], (L, FFN_HIDDEN, D_MODEL), FFN_HIDDEN ** -0.5),
        "norm_f_g": 1.0 + nrm(ks[24], (D_MODEL,), 0.01),
    }


def reference(x, norm1_g, w_in, gmlp_ln_g, gmlp_ln_b, gmlp_ws, gmlp_bs,
              s5_lambda_re, s5_lambda_im, s5_log_dt, s5_b_re, s5_b_im, s5_c_re, s5_c_im,
              s5_d, s5_w_glu, s5_b_glu, w_branch_a, w_branch_b, w_out,
              norm2_g, w_ffn_gate, w_ffn_up, w_ffn_down, norm_f_g):
    for l in range(DEPTH):
        x = hybrid_layer(x, norm1_g[l], w_in[l], gmlp_ln_g[l], gmlp_ln_b[l], gmlp_ws[l], gmlp_bs[l],
                         s5_lambda_re[l], s5_lambda_im[l], s5_log_dt[l], s5_b_re[l], s5_b_im[l],
                         s5_c_re[l], s5_c_im[l], s5_d[l], s5_w_glu[l], s5_b_glu[l],
                         w_branch_a[l], w_branch_b[l], w_out[l],
                         norm2_g[l], w_ffn_gate[l], w_ffn_up[l], w_ffn_down[l])
    return rms_norm(x, norm_f_g)
```

```python
import functools

import jax
import jax.numpy as jnp
import numpy as np
from jax import lax
from jax.experimental import pallas as pl
from jax.experimental.pallas import tpu as pltpu

D_MODEL = 1024
BATCH = 4
SEQ = 8192
CHUNK = 64
GMLP_BLOCK = 128
GMLP_GROUP_DIM = 128
GMLP_WIDTH = D_MODEL
GMLP_GROUPS = GMLP_WIDTH // GMLP_GROUP_DIM
S5_GROUP_DIM = 16
S5_WIDTH = D_MODEL // 2
S5_GROUPS = S5_WIDTH // S5_GROUP_DIM
S5_STATE = 64
IN_WIDTH = 2 * GMLP_WIDTH + S5_WIDTH + 2 * D_MODEL
FFN_HIDDEN = -(-8 * D_MODEL // 768) * 256
EPS = 1e-6

LANES = 128
SUBLANES = 8
S5_SLABS = S5_WIDTH // LANES
S5_SLAB_GROUPS = LANES // S5_GROUP_DIM
S5_SLAB_STATES = S5_SLAB_GROUPS * S5_STATE
S5_PAIR_COLS = 2 * 2 * S5_SLAB_STATES
S5_LAGS = 2
TT = GMLP_BLOCK
TM = BATCH * TT
ROWS_PER_STEP = 2 * SUBLANES
FFN_TM = 512
VMEM_LIMIT = 60 * 1024 * 1024

_GELU_C = float(np.sqrt(2.0 / np.pi))


def _gelu(x):
    return 0.5 * x * (1.0 + jnp.tanh(_GELU_C * (x + 0.044715 * (x * x * x))))


def _sigmoid(x):
    return 0.5 * jnp.tanh(0.5 * x) + 0.5


def _dot(a, b):
    return jnp.dot(a, b, preferred_element_type=jnp.float32)


def _mixer_kernel(x_ref, g1_ref, win_ref, lng_ref, lnb_ref, ws_ref, bs_ref,
                  w2_ref, a2r_ref, a2i_ref, cw_ref, d_ref, wglu_ref, bglu_ref,
                  pa_ref, pb_ref, wout_ref,
                  o_ref,
                  h_s, ug_s, vg_s, v_s, ya_s, uext_s, st_s, xs_s, hst_s, ypre_s, yslab_s, yb_s, mg_s):
    f32 = jnp.float32
    bf16 = jnp.bfloat16
    step = pl.program_id(0)

    @pl.when(step == 0)
    def _():
        hst_s[...] = jnp.zeros_like(hst_s)
        uext_s[:, 0:SUBLANES, :] = jnp.zeros((S5_SLABS, SUBLANES, LANES), f32)

    x = x_ref[...].reshape(TM, D_MODEL)
    ms = jnp.mean(x * x, axis=-1, keepdims=True)
    h_s[...] = (x * lax.rsqrt(ms + EPS) * g1_ref[...]).astype(bf16)

    s1 = GMLP_WIDTH
    s2 = 2 * GMLP_WIDTH
    s3 = s2 + S5_WIDTH
    s4 = s3 + D_MODEL
    cn = 512

    for c in range(0, GMLP_WIDTH, cn):
        ug_s[:, c:c + cn] = _gelu(_dot(h_s[...], win_ref[:, c:c + cn]))
    for c in range(0, GMLP_WIDTH, cn):
        vg_s[:, c:c + cn] = _gelu(_dot(h_s[...], win_ref[:, s1 + c:s1 + c + cn]))
    vg = vg_s[...]
    mu = jnp.mean(vg, axis=-1, keepdims=True)
    vc = vg - mu
    var = jnp.mean(vc * vc, axis=-1, keepdims=True)
    v_s[...] = (vc * lax.rsqrt(var + EPS) * lng_ref[...] + lnb_ref[...]).astype(bf16)

    xb = _dot(h_s[...], win_ref[:, s2:s3])
    for q in range(S5_SLABS):
        for b in range(BATCH):
            uext_s[q, pl.ds(SUBLANES + b, TT, stride=BATCH), :] = xb[b * TT:(b + 1) * TT, q * LANES:(q + 1) * LANES]

    for g in range(GMLP_GROUPS):
        cs = slice(g * GMLP_GROUP_DIM, (g + 1) * GMLP_GROUP_DIM)
        vcat = jnp.concatenate([v_s[b * TT:(b + 1) * TT, cs] for b in range(BATCH)], axis=1)
        mixed = _dot(ws_ref[g], vcat)
        for b in range(BATCH):
            rs = slice(b * TT, (b + 1) * TT)
            ya_s[rs, cs] = (ug_s[rs, cs] * (mixed[:, b * LANES:(b + 1) * LANES] + bs_ref[g])).astype(bf16)

    for m in range(S5_SLABS // 2):
        for ql in range(2):
            q = 2 * m + ql
            lag0 = uext_s[q, SUBLANES:SUBLANES + TM, :]
            lag1 = uext_s[q, SUBLANES - BATCH:SUBLANES - BATCH + TM, :]
            lhs = jnp.concatenate([lag0.astype(bf16), lag1.astype(bf16)], axis=1)
            st_s[:, ql * 2 * S5_SLAB_STATES:(ql + 1) * 2 * S5_SLAB_STATES] = _dot(lhs, w2_ref[q])

        a2r = jnp.broadcast_to(a2r_ref[m], (SUBLANES, S5_PAIR_COLS // 2))
        a2i = jnp.broadcast_to(a2i_ref[m], (SUBLANES, S5_PAIR_COLS // 2))
        ns = S5_SLAB_STATES

        def split(z):
            return (jnp.concatenate([z[:, 0:ns], z[:, 2 * ns:3 * ns]], axis=1),
                    jnp.concatenate([z[:, ns:2 * ns], z[:, 3 * ns:4 * ns]], axis=1))

        def merge(zr, zi):
            return jnp.concatenate([zr[:, 0:ns], zi[:, 0:ns], zr[:, ns:2 * ns], zi[:, ns:2 * ns]], axis=1)

        def scan_body(i, carry):
            xr, xi = carry
            r0 = pl.multiple_of(i * ROWS_PER_STEP, ROWS_PER_STEP)
            outs = []
            for k in range(ROWS_PER_STEP // SUBLANES):
                br, bi = split(st_s[pl.ds(r0 + k * SUBLANES, SUBLANES), :])
                nr = a2r * xr - a2i * xi + br
                ni = a2r * xi + a2i * xr + bi
                xr, xi = nr, ni
                outs.append(merge(xr, xi))
            xs_s[pl.ds(r0, ROWS_PER_STEP), :] = jnp.concatenate(outs, axis=0).astype(bf16)
            return xr, xi

        xr0, xi0 = split(hst_s[m])
        xr_f, xi_f = lax.fori_loop(0, TM // ROWS_PER_STEP, scan_body, (xr0, xi0))
        hst_s[m] = merge(xr_f, xi_f)

        y = _dot(xs_s[...], cw_ref[m])
        u_pair = jnp.concatenate([uext_s[2 * m, SUBLANES:SUBLANES + TM, :],
                                  uext_s[2 * m + 1, SUBLANES:SUBLANES + TM, :]], axis=1)
        ypre_s[:, m * 2 * LANES:(m + 1) * 2 * LANES] = y + d_ref[:, m * 2 * LANES:(m + 1) * 2 * LANES] * u_pair

    for q in range(S5_SLABS):
        uext_s[q, 0:SUBLANES, :] = uext_s[q, TM:TM + SUBLANES, :]

    yact = _gelu(ypre_s[...])
    z = _dot(yact.astype(bf16), wglu_ref[...]) + bglu_ref[...]
    ybp = yact * _sigmoid(z)
    for q in range(S5_SLABS):
        yslab_s[q] = ybp[:, q * LANES:(q + 1) * LANES]
    for q in range(S5_SLABS):
        for b in range(BATCH):
            yb_s[b * TT:(b + 1) * TT, q * LANES:(q + 1) * LANES] = (
                yslab_s[q, pl.ds(b, TT, stride=BATCH), :].astype(bf16))

    for c in range(0, D_MODEL, cn):
        ga = _dot(h_s[...], win_ref[:, s3 + c:s3 + c + cn])
        gb = _dot(h_s[...], win_ref[:, s4 + c:s4 + c + cn])
        ma = _dot(ya_s[...], pa_ref[:, c:c + cn])
        mb = _dot(yb_s[...], pb_ref[:, c:c + cn])
        mg_s[:, c:c + cn] = (_sigmoid(ga) * ma + _sigmoid(gb) * mb).astype(bf16)
    for c in range(0, D_MODEL, cn):
        res = x_ref[:, :, c:c + cn].reshape(TM, cn) + _dot(mg_s[...], wout_ref[:, c:c + cn])
        o_ref[:, :, c:c + cn] = res.reshape(BATCH, TT, cn)


def _ffn_kernel(x_ref, g2_ref, wg_ref, wu_ref, wd_ref, gf_ref, o_ref, h_s, hid_s, *, final_norm):
    bf16 = jnp.bfloat16
    x = x_ref[...]
    ms = jnp.mean(x * x, axis=-1, keepdims=True)
    h_s[...] = (x * lax.rsqrt(ms + EPS) * g2_ref[...]).astype(bf16)
    cn = 256
    for c in range(0, FFN_HIDDEN, cn):
        gate = _dot(h_s[...], wg_ref[:, c:c + cn])
        up = _dot(h_s[...], wu_ref[:, c:c + cn])
        hid_s[:, c:c + cn] = (gate * _sigmoid(gate) * up).astype(bf16)
    y = x_ref[...] + _dot(hid_s[...], wd_ref[...])
    if final_norm:
        ms2 = jnp.mean(y * y, axis=-1, keepdims=True)
        y = y * lax.rsqrt(ms2 + EPS) * gf_ref[...]
    o_ref[...] = y


def _resident(shape):
    nd = len(shape)
    return pl.BlockSpec(shape, lambda *_: (0,) * nd, pipeline_mode=pl.Buffered(1))


def _block_diag_slabs(w):
    g, a, b = w.shape
    w = w.reshape(S5_SLABS, S5_SLAB_GROUPS, a, b)
    eye = jnp.eye(S5_SLAB_GROUPS, dtype=w.dtype)
    out = jnp.einsum('qgab,gh->qgahb', w, eye)
    return out.reshape(S5_SLABS, S5_SLAB_GROUPS * a, S5_SLAB_GROUPS * b)


def _s5_params(lam_re, lam_im, log_dt, b_re, b_im, c_re, c_im):
    f32 = jnp.float32
    dt = jnp.exp(log_dt.astype(f32))[:, None]
    lr = lam_re.astype(f32)
    li = lam_im.astype(f32)
    mag = jnp.exp(lr * dt)
    ab_re = mag * jnp.cos(li * dt)
    ab_im = mag * jnp.sin(li * dt)
    den = lr * lr + li * li
    nr = ab_re - 1.0
    coef_re = (nr * lr + ab_im * li) / den
    coef_im = (ab_im * lr - nr * li) / den
    br = b_re.astype(f32)
    bi = b_im.astype(f32)
    bb_re = coef_re[..., None] * br - coef_im[..., None] * bi
    bb_im = coef_re[..., None] * bi + coef_im[..., None] * br
    ab1_re = ab_re[..., None] * bb_re - ab_im[..., None] * bb_im
    ab1_im = ab_re[..., None] * bb_im + ab_im[..., None] * bb_re
    a2_re = ab_re * ab_re - ab_im * ab_im
    a2_im = 2.0 * ab_re * ab_im

    def to_rows(w):
        return _block_diag_slabs(jnp.swapaxes(w, 1, 2))

    lag0 = jnp.concatenate([to_rows(bb_re), to_rows(bb_im)], axis=2)
    lag1 = jnp.concatenate([to_rows(ab1_re), to_rows(ab1_im)], axis=2)
    w2 = jnp.concatenate([lag0, lag1], axis=1).astype(jnp.bfloat16)

    a2r = a2_re.reshape(S5_SLABS // 2, 1, S5_PAIR_COLS // 2)
    a2i = a2_im.reshape(S5_SLABS // 2, 1, S5_PAIR_COLS // 2)

    c_slab = jnp.concatenate([_block_diag_slabs(jnp.swapaxes(c_re.astype(f32), 1, 2)),
                              -_block_diag_slabs(jnp.swapaxes(c_im.astype(f32), 1, 2))], axis=1)
    zeros = jnp.zeros_like(c_slab)
    cw = jnp.concatenate([jnp.concatenate([c_slab[0::2], zeros[0::2]], axis=2),
                          jnp.concatenate([zeros[1::2], c_slab[1::2]], axis=2)], axis=1)
    return w2, a2r, a2i, cw.astype(jnp.bfloat16)


def _mixer(x, g1, win, lng, lnb, ws, bs, w2, a2r, a2i, cw, d, wglu, bglu, pa, pb, wout):
    f32 = jnp.float32
    bf16 = jnp.bfloat16
    weights = (g1, win, lng, lnb, ws, bs, w2, a2r, a2i, cw, d, wglu, bglu, pa, pb, wout)
    x_spec = pl.BlockSpec((BATCH, TT, D_MODEL), lambda i: (0, i, 0))
    return pl.pallas_call(
        _mixer_kernel,
        grid=(SEQ // TT,),
        in_specs=[x_spec] + [_resident(w.shape) for w in weights],
        out_specs=x_spec,
        out_shape=jax.ShapeDtypeStruct((BATCH, SEQ, D_MODEL), f32),
        scratch_shapes=[
            pltpu.VMEM((TM, D_MODEL), bf16),
            pltpu.VMEM((TM, GMLP_WIDTH), f32),
            pltpu.VMEM((TM, GMLP_WIDTH), f32),
            pltpu.VMEM((TM, GMLP_WIDTH), bf16),
            pltpu.VMEM((TM, GMLP_WIDTH), bf16),
            pltpu.VMEM((S5_SLABS, TM + SUBLANES, LANES), f32),
            pltpu.VMEM((TM, S5_PAIR_COLS), f32),
            pltpu.VMEM((TM, S5_PAIR_COLS), bf16),
            pltpu.VMEM((S5_SLABS // 2, SUBLANES, S5_PAIR_COLS), f32),
            pltpu.VMEM((TM, S5_WIDTH), f32),
            pltpu.VMEM((S5_SLABS, TM, LANES), f32),
            pltpu.VMEM((TM, S5_WIDTH), bf16),
            pltpu.VMEM((TM, D_MODEL), bf16),
        ],
        compiler_params=pltpu.CompilerParams(
            dimension_semantics=("arbitrary",), vmem_limit_bytes=VMEM_LIMIT),
        name="mixer",
    )(x, *weights)


def _ffn(x, g2, wg, wu, wd, gf, final_norm):
    f32 = jnp.float32
    n = x.shape[0]
    weights = (g2, wg, wu, wd, gf)
    x_spec = pl.BlockSpec((FFN_TM, D_MODEL), lambda i: (i, 0))
    return pl.pallas_call(
        functools.partial(_ffn_kernel, final_norm=final_norm),
        grid=(n // FFN_TM,),
        in_specs=[x_spec] + [_resident(w.shape) for w in weights],
        out_specs=x_spec,
        out_shape=jax.ShapeDtypeStruct((n, D_MODEL), f32),
        scratch_shapes=[
            pltpu.VMEM((FFN_TM, D_MODEL), jnp.bfloat16),
            pltpu.VMEM((FFN_TM, FFN_HIDDEN), jnp.bfloat16),
        ],
        compiler_params=pltpu.CompilerParams(
            dimension_semantics=("arbitrary",), vmem_limit_bytes=VMEM_LIMIT),
        name="ffn",
    )(x, *weights)


def kernel(x, norm1_g, w_in, gmlp_ln_g, gmlp_ln_b, gmlp_ws, gmlp_bs, s5_lambda_re, s5_lambda_im, s5_log_dt, s5_b_re, s5_b_im, s5_c_re, s5_c_im, s5_d, s5_w_glu, s5_b_glu, w_branch_a, w_branch_b, w_out, norm2_g, w_ffn_gate, w_ffn_up, w_ffn_down, norm_f_g):
    f32 = jnp.float32
    bf16 = jnp.bfloat16
    depth = norm1_g.shape[0]
    cidx = jnp.arange(GMLP_BLOCK) // CHUNK
    mask = cidx[None, :] <= cidx[:, None]
    row = lambda v: v.astype(f32).reshape(1, -1)
    for l in range(depth):
        ws = jnp.where(mask[None], gmlp_ws[l], jnp.zeros_like(gmlp_ws[l])).astype(bf16)
        bs = jnp.broadcast_to(gmlp_bs[l].astype(f32)[:, :, None], (GMLP_GROUPS, GMLP_BLOCK, LANES))
        w2, a2r, a2i, cw = _s5_params(s5_lambda_re[l], s5_lambda_im[l], s5_log_dt[l],
                                      s5_b_re[l], s5_b_im[l], s5_c_re[l], s5_c_im[l])
        x = _mixer(x, row(norm1_g[l]), w_in[l].astype(bf16), row(gmlp_ln_g[l]), row(gmlp_ln_b[l]), ws, bs,
                   w2, a2r, a2i, cw, row(s5_d[l]), s5_w_glu[l].astype(bf16), row(s5_b_glu[l]),
                   w_branch_a[l].astype(bf16), w_branch_b[l].astype(bf16), w_out[l].astype(bf16))
        y = _ffn(x.reshape(BATCH * SEQ, D_MODEL), row(norm2_g[l]), w_ffn_gate[l].astype(bf16),
                 w_ffn_up[l].astype(bf16), w_ffn_down[l].astype(bf16), row(norm_f_g), l == depth - 1)
        x = y.reshape(BATCH, SEQ, D_MODEL)
    return x
```

```python
import functools

import jax
import jax.numpy as jnp
import numpy as np
from jax import lax
from jax.experimental import pallas as pl
from jax.experimental.pallas import tpu as pltpu

D_MODEL = 1024
BATCH = 4
SEQ = 8192
CHUNK = 64
GMLP_BLOCK = 128
GMLP_GROUP_DIM = 128
GMLP_WIDTH = D_MODEL
GMLP_GROUPS = GMLP_WIDTH // GMLP_GROUP_DIM
S5_GROUP_DIM = 16
S5_WIDTH = D_MODEL // 2
S5_GROUPS = S5_WIDTH // S5_GROUP_DIM
S5_STATE = 64
IN_WIDTH = 2 * GMLP_WIDTH + S5_WIDTH + 2 * D_MODEL
FFN_HIDDEN = -(-8 * D_MODEL // 768) * 256
EPS = 1e-6

LANES = 128
SUBLANES = 8
S5_SLABS = S5_WIDTH // LANES
S5_SLAB_GROUPS = LANES // S5_GROUP_DIM
S5_SLAB_STATES = S5_SLAB_GROUPS * S5_STATE
S5_PAIR_COLS = 2 * 2 * S5_SLAB_STATES
S5_LAGS = 2
TT = GMLP_BLOCK
TM = BATCH * TT
ROWS_PER_STEP = 2 * SUBLANES
FFN_TM = 512
VMEM_LIMIT = 60 * 1024 * 1024

_GELU_C = float(np.sqrt(2.0 / np.pi))


def _gelu(x):
    return 0.5 * x * (1.0 + jnp.tanh(_GELU_C * (x + 0.044715 * (x * x * x))))


def _sigmoid(x):
    return 0.5 * jnp.tanh(0.5 * x) + 0.5


def _mixer_kernel(x_ref, g1_ref, win_ref, lng_ref, lnb_ref, ws_ref, bs_ref,
                  w2_ref, a2r_ref, a2i_ref, cw_ref, d_ref, wglu_ref, bglu_ref,
                  pa_ref, pb_ref, wout_ref,
                  o_ref,
                  h_s, ug_s, vg_s, v_s, ya_s, uext_s, st_s, xs_s, hst_s, ypre_s, yslab_s, yb_s, sga_s, mg_s):
    f32 = jnp.float32
    bf16 = jnp.bfloat16
    step = pl.program_id(0)

    @pl.when(step == 0)
    def _():
        hst_s[...] = jnp.zeros_like(hst_s)
        uext_s[:, 0:SUBLANES, :] = jnp.zeros((S5_SLABS, SUBLANES, LANES), f32)

    s1 = GMLP_WIDTH
    s2 = 2 * GMLP_WIDTH
    s3 = s2 + S5_WIDTH
    s4 = s3 + D_MODEL
    cn = 256
    ns = S5_SLAB_STATES
    n_parts = 4
    steps_per_part = TM // ROWS_PER_STEP // n_parts

    def rmsnorm():
        x = x_ref[...].reshape(TM, D_MODEL)
        ms = jnp.mean(x * x, axis=-1, keepdims=True)
        h_s[...] = (x * lax.rsqrt(ms + EPS) * g1_ref[...]).astype(bf16)

    def u_chunk(c):
        ug_s[:, c:c + cn] = _gelu(jnp.dot(h_s[...], win_ref[:, c:c + cn], preferred_element_type=f32))

    def v_chunk(c):
        vg_s[:, c:c + cn] = _gelu(jnp.dot(h_s[...], win_ref[:, s1 + c:s1 + c + cn], preferred_element_type=f32))

    def layer_norm():
        vg = vg_s[...]
        mu = jnp.mean(vg, axis=-1, keepdims=True)
        vc = vg - mu
        var = jnp.mean(vc * vc, axis=-1, keepdims=True)
        v_s[...] = (vc * lax.rsqrt(var + EPS) * lng_ref[...] + lnb_ref[...]).astype(bf16)

    def xb_proj():
        xb = jnp.dot(h_s[...], win_ref[:, s2:s3], preferred_element_type=f32)
        for q in range(S5_SLABS):
            for b in range(BATCH):
                uext_s[q, pl.ds(SUBLANES + b, TT, stride=BATCH), :] = (
                    xb[b * TT:(b + 1) * TT, q * LANES:(q + 1) * LANES])

    def bu(q):
        m, ql = divmod(q, 2)
        lag0 = uext_s[q, SUBLANES:SUBLANES + TM, :]
        lag1 = uext_s[q, SUBLANES - BATCH:SUBLANES - BATCH + TM, :]
        lhs = jnp.concatenate([lag0.astype(bf16), lag1.astype(bf16)], axis=1)
        st_s[m, :, ql * 2 * ns:(ql + 1) * 2 * ns] = jnp.dot(lhs, w2_ref[q], preferred_element_type=f32)

    def split(z):
        return (jnp.concatenate([z[:, 0:ns], z[:, 2 * ns:3 * ns]], axis=1),
                jnp.concatenate([z[:, ns:2 * ns], z[:, 3 * ns:4 * ns]], axis=1))

    def merge(zr, zi):
        return jnp.concatenate([zr[:, 0:ns], zi[:, 0:ns], zr[:, ns:2 * ns], zi[:, ns:2 * ns]], axis=1)

    carry = {}

    def scan_part(m, p):
        xr, xi = split(hst_s[m]) if p == 0 else carry[m]
        for i in range(p * steps_per_part, (p + 1) * steps_per_part):
            r0 = i * ROWS_PER_STEP
            outs = []
            for k in range(ROWS_PER_STEP // SUBLANES):
                br, bi = split(st_s[m, r0 + k * SUBLANES:r0 + (k + 1) * SUBLANES, :])
                a2r = a2r_ref[m]
                a2i = a2i_ref[m]
                xr, xi = a2r * xr - a2i * xi + br, a2r * xi + a2i * xr + bi
                outs.append(merge(xr, xi))
            xs_s[m, r0:r0 + ROWS_PER_STEP, :] = jnp.concatenate(outs, axis=0).astype(bf16)
        carry[m] = (xr, xi)
        if p == n_parts - 1:
            hst_s[m] = merge(xr, xi)

    def cy(m):
        y = jnp.dot(xs_s[m], cw_ref[m], preferred_element_type=f32)
        u_pair = jnp.concatenate([uext_s[2 * m, SUBLANES:SUBLANES + TM, :],
                                  uext_s[2 * m + 1, SUBLANES:SUBLANES + TM, :]], axis=1)
        ypre_s[:, m * 2 * LANES:(m + 1) * 2 * LANES] = y + d_ref[:, m * 2 * LANES:(m + 1) * 2 * LANES] * u_pair

    def ga_chunk(c):
        sga_s[:, c:c + cn] = _sigmoid(jnp.dot(h_s[...], win_ref[:, s3 + c:s3 + c + cn], preferred_element_type=f32))

    def glu():
        yact = _gelu(ypre_s[...])
        z = jnp.dot(yact.astype(bf16), wglu_ref[...], preferred_element_type=f32) + bglu_ref[...]
        ybp = yact * _sigmoid(z)
        for q in range(S5_SLABS):
            yslab_s[q] = ybp[:, q * LANES:(q + 1) * LANES]
        for q in range(S5_SLABS):
            for b in range(BATCH):
                yb_s[b * TT:(b + 1) * TT, q * LANES:(q + 1) * LANES] = (
                    yslab_s[q, pl.ds(b, TT, stride=BATCH), :].astype(bf16))

    def gmlp(g):
        cs = slice(g * GMLP_GROUP_DIM, (g + 1) * GMLP_GROUP_DIM)
        vcat = jnp.concatenate([v_s[b * TT:(b + 1) * TT, cs] for b in range(BATCH)], axis=1)
        mixed = jnp.dot(ws_ref[g], vcat, preferred_element_type=f32)
        for b in range(BATCH):
            rs = slice(b * TT, (b + 1) * TT)
            ya_s[rs, cs] = (ug_s[rs, cs] * (mixed[:, b * LANES:(b + 1) * LANES] + bs_ref[g])).astype(bf16)

    def merge_chunk(c):
        gb = jnp.dot(h_s[...], win_ref[:, s4 + c:s4 + c + cn], preferred_element_type=f32)
        ma = jnp.dot(ya_s[...], pa_ref[:, c:c + cn], preferred_element_type=f32)
        mb = jnp.dot(yb_s[...], pb_ref[:, c:c + cn], preferred_element_type=f32)
        mg_s[:, c:c + cn] = (sga_s[:, c:c + cn] * ma + _sigmoid(gb) * mb).astype(bf16)

    def out_chunk(c):
        res = x_ref[:, :, c:c + cn].reshape(TM, cn) + jnp.dot(mg_s[...], wout_ref[:, c:c + cn],
                                                              preferred_element_type=f32)
        o_ref[:, :, c:c + cn] = res.reshape(BATCH, TT, cn)

    rmsnorm()
    xb_proj()
    u_chunk(0 * cn)
    bu(0)
    bu(1)
    u_chunk(1 * cn)
    bu(2)
    scan_part(0, 0)
    bu(3)
    scan_part(0, 1)
    u_chunk(2 * cn)
    scan_part(0, 2)
    ga_chunk(0 * cn)
    u_chunk(3 * cn)
    scan_part(0, 3)
    cy(0)
    scan_part(1, 0)
    v_chunk(0 * cn)
    ga_chunk(1 * cn)
    scan_part(1, 1)
    v_chunk(1 * cn)
    ga_chunk(2 * cn)
    scan_part(1, 2)
    v_chunk(2 * cn)
    ga_chunk(3 * cn)
    scan_part(1, 3)
    v_chunk(3 * cn)
    cy(1)
    layer_norm()
    for q in range(S5_SLABS):
        uext_s[q, 0:SUBLANES, :] = uext_s[q, TM:TM + SUBLANES, :]
    glu()
    for g in range(GMLP_GROUPS):
        gmlp(g)
    for c in range(0, D_MODEL, cn):
        merge_chunk(c)
    for c in range(0, D_MODEL, cn):
        out_chunk(c)


def _ffn_kernel(x_ref, g2_ref, wg_ref, wu_ref, wd_ref, gf_ref, o_ref, h_s, hid_s, *, final_norm):
    bf16 = jnp.bfloat16
    x = x_ref[...]
    ms = jnp.mean(x * x, axis=-1, keepdims=True)
    h_s[...] = (x * lax.rsqrt(ms + EPS) * g2_ref[...]).astype(bf16)
    cn = 256
    for c in range(0, FFN_HIDDEN, cn):
        gate = jnp.dot(h_s[...], wg_ref[:, c:c + cn], preferred_element_type=jnp.float32)
        up = jnp.dot(h_s[...], wu_ref[:, c:c + cn], preferred_element_type=jnp.float32)
        hid_s[:, c:c + cn] = (gate * _sigmoid(gate) * up).astype(bf16)
    y = x_ref[...] + jnp.dot(hid_s[...], wd_ref[...], preferred_element_type=jnp.float32)
    if final_norm:
        ms2 = jnp.mean(y * y, axis=-1, keepdims=True)
        y = y * lax.rsqrt(ms2 + EPS) * gf_ref[...]
    o_ref[...] = y


def _resident(shape):
    nd = len(shape)
    return pl.BlockSpec(shape, lambda *_: (0,) * nd, pipeline_mode=pl.Buffered(1))


def _block_diag_slabs(w):
    g, a, b = w.shape
    w = w.reshape(S5_SLABS, S5_SLAB_GROUPS, a, b)
    eye = jnp.eye(S5_SLAB_GROUPS, dtype=w.dtype)
    out = jnp.einsum('qgab,gh->qgahb', w, eye)
    return out.reshape(S5_SLABS, S5_SLAB_GROUPS * a, S5_SLAB_GROUPS * b)


def _s5_params(lam_re, lam_im, log_dt, b_re, b_im, c_re, c_im):
    f32 = jnp.float32
    dt = jnp.exp(log_dt.astype(f32))[:, None]
    lr = lam_re.astype(f32)
    li = lam_im.astype(f32)
    mag = jnp.exp(lr * dt)
    ab_re = mag * jnp.cos(li * dt)
    ab_im = mag * jnp.sin(li * dt)
    den = lr * lr + li * li
    nr = ab_re - 1.0
    coef_re = (nr * lr + ab_im * li) / den
    coef_im = (ab_im * lr - nr * li) / den
    br = b_re.astype(f32)
    bi = b_im.astype(f32)
    bb_re = coef_re[..., None] * br - coef_im[..., None] * bi
    bb_im = coef_re[..., None] * bi + coef_im[..., None] * br
    ab1_re = ab_re[..., None] * bb_re - ab_im[..., None] * bb_im
    ab1_im = ab_re[..., None] * bb_im + ab_im[..., None] * bb_re
    a2_re = ab_re * ab_re - ab_im * ab_im
    a2_im = 2.0 * ab_re * ab_im

    def to_rows(w):
        return _block_diag_slabs(jnp.swapaxes(w, 1, 2))

    lag0 = jnp.concatenate([to_rows(bb_re), to_rows(bb_im)], axis=2)
    lag1 = jnp.concatenate([to_rows(ab1_re), to_rows(ab1_im)], axis=2)
    w2 = jnp.concatenate([lag0, lag1], axis=1).astype(jnp.bfloat16)

    bshape = (S5_SLABS // 2, SUBLANES, S5_PAIR_COLS // 2)
    a2r = jnp.broadcast_to(a2_re.reshape(S5_SLABS // 2, 1, S5_PAIR_COLS // 2), bshape)
    a2i = jnp.broadcast_to(a2_im.reshape(S5_SLABS // 2, 1, S5_PAIR_COLS // 2), bshape)

    c_slab = jnp.concatenate([_block_diag_slabs(jnp.swapaxes(c_re.astype(f32), 1, 2)),
                              -_block_diag_slabs(jnp.swapaxes(c_im.astype(f32), 1, 2))], axis=1)
    zeros = jnp.zeros_like(c_slab)
    cw = jnp.concatenate([jnp.concatenate([c_slab[0::2], zeros[0::2]], axis=2),
                          jnp.concatenate([zeros[1::2], c_slab[1::2]], axis=2)], axis=1)
    return w2, a2r, a2i, cw.astype(jnp.bfloat16)


def _mixer(x, g1, win, lng, lnb, ws, bs, w2, a2r, a2i, cw, d, wglu, bglu, pa, pb, wout):
    f32 = jnp.float32
    bf16 = jnp.bfloat16
    weights = (g1, win, lng, lnb, ws, bs, w2, a2r, a2i, cw, d, wglu, bglu, pa, pb, wout)
    x_spec = pl.BlockSpec((BATCH, TT, D_MODEL), lambda i: (0, i, 0))
    return pl.pallas_call(
        _mixer_kernel,
        grid=(SEQ // TT,),
        in_specs=[x_spec] + [_resident(w.shape) for w in weights],
        out_specs=x_spec,
        out_shape=jax.ShapeDtypeStruct((BATCH, SEQ, D_MODEL), f32),
        scratch_shapes=[
            pltpu.VMEM((TM, D_MODEL), bf16),
            pltpu.VMEM((TM, GMLP_WIDTH), f32),
            pltpu.VMEM((TM, GMLP_WIDTH), f32),
            pltpu.VMEM((TM, GMLP_WIDTH), bf16),
            pltpu.VMEM((TM, GMLP_WIDTH), bf16),
            pltpu.VMEM((S5_SLABS, TM + SUBLANES, LANES), f32),
            pltpu.VMEM((S5_SLABS // 2, TM, S5_PAIR_COLS), f32),
            pltpu.VMEM((S5_SLABS // 2, TM, S5_PAIR_COLS), bf16),
            pltpu.VMEM((S5_SLABS // 2, SUBLANES, S5_PAIR_COLS), f32),
            pltpu.VMEM((TM, S5_WIDTH), f32),
            pltpu.VMEM((S5_SLABS, TM, LANES), f32),
            pltpu.VMEM((TM, S5_WIDTH), bf16),
            pltpu.VMEM((TM, D_MODEL), f32),
            pltpu.VMEM((TM, D_MODEL), bf16),
        ],
        compiler_params=pltpu.CompilerParams(
            dimension_semantics=("arbitrary",), vmem_limit_bytes=VMEM_LIMIT),
        name="mixer",
    )(x, *weights)


def _ffn(x, g2, wg, wu, wd, gf, final_norm):
    f32 = jnp.float32
    n = x.shape[0]
    weights = (g2, wg, wu, wd, gf)
    x_spec = pl.BlockSpec((FFN_TM, D_MODEL), lambda i: (i, 0))
    return pl.pallas_call(
        functools.partial(_ffn_kernel, final_norm=final_norm),
        grid=(n // FFN_TM,),
        in_specs=[x_spec] + [_resident(w.shape) for w in weights],
        out_specs=x_spec,
        out_shape=jax.ShapeDtypeStruct((n, D_MODEL), f32),
        scratch_shapes=[
            pltpu.VMEM((FFN_TM, D_MODEL), jnp.bfloat16),
            pltpu.VMEM((FFN_TM, FFN_HIDDEN), jnp.bfloat16),
        ],
        compiler_params=pltpu.CompilerParams(
            dimension_semantics=("arbitrary",), vmem_limit_bytes=VMEM_LIMIT),
        name="ffn",
    )(x, *weights)


def kernel(x, norm1_g, w_in, gmlp_ln_g, gmlp_ln_b, gmlp_ws, gmlp_bs, s5_lambda_re, s5_lambda_im, s5_log_dt, s5_b_re, s5_b_im, s5_c_re, s5_c_im, s5_d, s5_w_glu, s5_b_glu, w_branch_a, w_branch_b, w_out, norm2_g, w_ffn_gate, w_ffn_up, w_ffn_down, norm_f_g):
    f32 = jnp.float32
    bf16 = jnp.bfloat16
    depth = norm1_g.shape[0]
    cidx = jnp.arange(GMLP_BLOCK) // CHUNK
    mask = cidx[None, :] <= cidx[:, None]
    row = lambda v: v.astype(f32).reshape(1, -1)
    for l in range(depth):
        ws = jnp.where(mask[None], gmlp_ws[l], jnp.zeros_like(gmlp_ws[l])).astype(bf16)
        bs = jnp.broadcast_to(gmlp_bs[l].astype(f32)[:, :, None], (GMLP_GROUPS, GMLP_BLOCK, LANES))
        w2, a2r, a2i, cw = _s5_params(s5_lambda_re[l], s5_lambda_im[l], s5_log_dt[l],
                                      s5_b_re[l], s5_b_im[l], s5_c_re[l], s5_c_im[l])
        x = _mixer(x, row(norm1_g[l]), w_in[l].astype(bf16), row(gmlp_ln_g[l]), row(gmlp_ln_b[l]), ws, bs,
                   w2, a2r, a2i, cw, row(s5_d[l]), s5_w_glu[l].astype(bf16), row(s5_b_glu[l]),
                   w_branch_a[l].astype(bf16), w_branch_b[l].astype(bf16), w_out[l].astype(bf16))
        y = _ffn(x.reshape(BATCH * SEQ, D_MODEL), row(norm2_g[l]), w_ffn_gate[l].astype(bf16),
                 w_ffn_up[l].astype(bf16), w_ffn_down[l].astype(bf16), row(norm_f_g), l == depth - 1)
        x = y.reshape(BATCH, SEQ, D_MODEL)
    return x
```

```python
import functools

import jax
import jax.numpy as jnp
import numpy as np
from jax import lax
from jax.experimental import pallas as pl
from jax.experimental.pallas import tpu as pltpu

D_MODEL = 1024
BATCH = 4
SEQ = 8192
CHUNK = 64
GMLP_BLOCK = 128
GMLP_GROUP_DIM = 128
GMLP_WIDTH = D_MODEL
GMLP_GROUPS = GMLP_WIDTH // GMLP_GROUP_DIM
S5_GROUP_DIM = 16
S5_WIDTH = D_MODEL // 2
S5_STATE = 64
FFN_HIDDEN = -(-8 * D_MODEL // 768) * 256
EPS = 1e-6

LANES = 128
SUBLANES = 8
BF16_ROWS = 2 * SUBLANES
S5_SLABS = S5_WIDTH // LANES
S5_SLAB_GROUPS = LANES // S5_GROUP_DIM
S5_SLAB_STATES = S5_SLAB_GROUPS * S5_STATE
S5_PAIR_COLS = 2 * 2 * S5_SLAB_STATES
S5_LAGS = 4
S5_HALVES = 2
S5_HALF_GROUPS = S5_SLAB_GROUPS // S5_HALVES
S5_HALF_STATES = S5_HALF_GROUPS * S5_STATE
S5_PARITIES = 2
TT = GMLP_BLOCK
TM = BATCH * TT
HM = TM // S5_PARITIES
ROWS_PER_STEP = 2 * SUBLANES
FFN_TM = 1024
PREP_STEPS = 8
VMEM_LIMIT = 60 * 1024 * 1024

_GELU_C = float(np.sqrt(2.0 / np.pi))
_LOG2_GROUP_DIM = S5_GROUP_DIM.bit_length() - 1
_LOG2_STATE = S5_STATE.bit_length() - 1
assert 1 << _LOG2_GROUP_DIM == S5_GROUP_DIM and 1 << _LOG2_STATE == S5_STATE


def _gelu(x):
    hx = 0.5 * x
    return hx * jnp.tanh(x * ((x * x) * (_GELU_C * 0.044715) + _GELU_C)) + hx


def _twice_sigmoid_of_twice(x_half):
    return jnp.tanh(x_half) + 1.0


def _mixer_kernel(x_ref, g1_ref, lng_ref, lnb_ref, ws_ref, bs_ref,
                  bc_ref, cc_ref, kc_ref, tw_ref, tc_ref, tk_ref, a4r_ref, a4i_ref, d_ref, bglu_ref,
                  win_f, wglu_f, pa_f, pb_f, wout_f, wg_ref, wu_ref, wd_ref,
                  o_ref, wg_o, wu_o, wd_o,
                  win_ref, wglu_ref, pa_ref, pb_ref, wout_ref, w4_s, cw_s, k0_s, h_s, ug_s, vg_s, v_s, ya_s, xbs_s, uext_s, st_s, xs_s, hst_s, ypre_s, yslab_s, ybm_s, yb_s,
                  sga_s, mg_s):
    f32 = jnp.float32
    bf16 = jnp.bfloat16
    step = pl.program_id(0)

    def block_diag(compact, tiler, row_group, col_group):
        full = jnp.dot(compact, tiler, preferred_element_type=f32)
        rg = row_group(lax.broadcasted_iota(jnp.int32, full.shape, 0))
        cg = col_group(lax.broadcasted_iota(jnp.int32, full.shape, 1))
        return jnp.where(rg == cg, full, 0.0).astype(bf16)

    s1 = GMLP_WIDTH
    s2 = 2 * GMLP_WIDTH
    s3 = s2 + S5_WIDTH
    s4 = s3 + D_MODEL

    @pl.when(step < PREP_STEPS)
    def _():
        def rows(ref):
            n = ref.shape[0]
            return pl.ds(pl.multiple_of(step * n, n), n)
        gate_col = lax.broadcasted_iota(jnp.int32, win_f.shape, 1) >= s3
        win_ref[rows(win_f), :] = (win_f[...] * jnp.where(gate_col, 0.5, 1.0)).astype(bf16)
        wglu_ref[rows(wglu_f), :] = (0.5 * wglu_f[...]).astype(bf16)
        pa_ref[rows(pa_f), :] = pa_f[...].astype(bf16)
        pb_ref[rows(pb_f), :] = (0.5 * pb_f[...]).astype(bf16)
        wout_ref[rows(wout_f), :] = (0.5 * wout_f[...]).astype(bf16)

    @pl.when(step == PREP_STEPS)
    def _():
        hst_s[...] = jnp.zeros_like(hst_s)
        uext_s[:, :, 0:SUBLANES, :] = jnp.zeros((S5_SLABS, S5_PARITIES, SUBLANES, LANES), f32)
        sg, hg = S5_SLAB_GROUPS, S5_HALF_GROUPS
        lg_half = S5_HALF_STATES.bit_length() - 1
        for q in range(S5_SLABS):
            for hf in range(S5_HALVES):
                w4_s[q, hf] = block_diag(bc_ref[q, hf], tw_ref[...],
                                         lambda r: (r >> _LOG2_GROUP_DIM) & (hg - 1),
                                         lambda c: (c >> _LOG2_STATE) & (hg - 1))
            cw_s[q] = block_diag(cc_ref[q], tc_ref[...],
                                 lambda r: ((r >> (lg_half + 1)) * hg) + ((r >> _LOG2_STATE) & (hg - 1)),
                                 lambda c: (c >> _LOG2_GROUP_DIM) & (sg - 1))
        for m in range(S5_SLABS // 2):
            k0_s[m] = block_diag(kc_ref[m], tk_ref[...],
                                 lambda r: r >> _LOG2_GROUP_DIM, lambda c: c >> _LOG2_GROUP_DIM)

    cn = 256
    ns = S5_SLAB_STATES
    hs = S5_HALF_STATES
    n_parts = 4
    steps_per_part = HM // ROWS_PER_STEP // n_parts
    frames = TT // S5_PARITIES

    def rmsnorm():
        x = x_ref[...].reshape(TM, D_MODEL)
        ms = jnp.mean(x * x, axis=-1, keepdims=True)
        h_s[...] = (x * lax.rsqrt(ms + EPS) * g1_ref[...]).astype(bf16)

    def u_chunk(c):
        ug_s[:, c:c + cn] = _gelu(jnp.dot(h_s[...], win_ref[:, c:c + cn], preferred_element_type=f32))

    def v_chunk(c):
        vg_s[:, c:c + cn] = _gelu(jnp.dot(h_s[...], win_ref[:, s1 + c:s1 + c + cn], preferred_element_type=f32))

    def layer_norm():
        vg = vg_s[...]
        mu = jnp.mean(vg, axis=-1, keepdims=True)
        vc = vg - mu
        var = jnp.mean(vc * vc, axis=-1, keepdims=True)
        v_s[...] = (vc * lax.rsqrt(var + EPS) * lng_ref[...] + lnb_ref[...]).astype(bf16)

    def xb_proj():
        xb = jnp.dot(h_s[...], win_ref[:, s2:s3], preferred_element_type=f32)
        for q in range(S5_SLABS):
            xbs_s[q] = xb[:, q * LANES:(q + 1) * LANES]
        for q in range(S5_SLABS):
            for r in range(S5_PARITIES):
                for b in range(BATCH):
                    uext_s[q, r, pl.ds(SUBLANES + b, frames, stride=BATCH), :] = (
                        xbs_s[q, pl.ds(b * TT + r, frames, stride=S5_PARITIES), :])

    def bu(q):
        m, ql = divmod(q, 2)
        lags = []
        for d in range(S5_LAGS):
            off = SUBLANES - BATCH * ((d + 1) // 2)
            lags.append(uext_s[q, d % 2, off:off + HM, :])
        low = lax.broadcasted_iota(jnp.int32, (HM, LANES), 1) < LANES // 2
        for hf in range(S5_HALVES):
            tiles = []
            for d in range(0, S5_LAGS, 2):
                if hf == 0:
                    t = jnp.where(low, lags[d], pltpu.roll(lags[d + 1], LANES // 2, 1))
                else:
                    t = jnp.where(low, pltpu.roll(lags[d], LANES // 2, 1), lags[d + 1])
                tiles.append(t.astype(bf16))
            lhs = jnp.concatenate(tiles, axis=1)
            c0 = ql * 2 * ns + hf * 2 * hs
            st_s[m, :, c0:c0 + 2 * hs] = jnp.dot(lhs, w4_s[q, hf], preferred_element_type=f32)

    def split(z):
        res = [z[:, k * 2 * hs:k * 2 * hs + hs] for k in range(2 * S5_HALVES)]
        ims = [z[:, k * 2 * hs + hs:(k + 1) * 2 * hs] for k in range(2 * S5_HALVES)]
        return jnp.concatenate(res, axis=1), jnp.concatenate(ims, axis=1)

    def merge(zr, zi):
        parts = []
        for k in range(2 * S5_HALVES):
            parts += [zr[:, k * hs:(k + 1) * hs], zi[:, k * hs:(k + 1) * hs]]
        return jnp.concatenate(parts, axis=1)

    carry = {}

    def scan_part(m, p):
        xr, xi = split(hst_s[m]) if p == 0 else carry[m]
        for i in range(p * steps_per_part, (p + 1) * steps_per_part):
            r0 = i * ROWS_PER_STEP
            outs = []
            for k in range(ROWS_PER_STEP // SUBLANES):
                br, bi = split(st_s[m, r0 + k * SUBLANES:r0 + (k + 1) * SUBLANES, :])
                a4r = a4r_ref[m]
                a4i = a4i_ref[m]
                xr, xi = a4r * xr - a4i * xi + br, a4r * xi + a4i * xr + bi
                outs.append(merge(xr, xi))
            xs_s[m, r0:r0 + ROWS_PER_STEP, :] = jnp.concatenate(outs, axis=0).astype(bf16)
        carry[m] = (xr, xi)
        if p == n_parts - 1:
            hst_s[m] = merge(xr, xi)

    def cur(q, r):
        return uext_s[q, r, SUBLANES:SUBLANES + HM, :]

    def cy(m):
        u_odd = jnp.concatenate([cur(2 * m, 1), cur(2 * m + 1, 1)], axis=1)
        direct = jnp.dot(u_odd.astype(bf16), k0_s[m], preferred_element_type=f32)
        for ql in range(2):
            q = 2 * m + ql
            cs = slice(q * LANES, (q + 1) * LANES)
            y2 = jnp.dot(xs_s[m, :, ql * 2 * ns:(ql + 1) * 2 * ns], cw_s[q], preferred_element_type=f32)
            ypre_s[0:HM, cs] = y2[:, 0:LANES] + d_ref[:, cs] * cur(q, 0)
            ypre_s[HM:TM, cs] = (y2[:, LANES:2 * LANES] + direct[:, ql * LANES:(ql + 1) * LANES]
                                 + d_ref[:, cs] * cur(q, 1))

    def ga_chunk(c):
        sga_s[:, c:c + cn] = _twice_sigmoid_of_twice(
            jnp.dot(h_s[...], win_ref[:, s3 + c:s3 + c + cn], preferred_element_type=f32))

    def glu():
        yact = _gelu(ypre_s[...])
        z = jnp.dot(yact.astype(bf16), wglu_ref[...], preferred_element_type=f32) + bglu_ref[...]
        ybp = yact * _twice_sigmoid_of_twice(z)
        for q in range(S5_SLABS):
            yslab_s[q] = ybp[:, q * LANES:(q + 1) * LANES]
        for q in range(S5_SLABS):
            for r in range(S5_PARITIES):
                for b in range(BATCH):
                    ybm_s[q, pl.ds(b * TT + r, frames, stride=S5_PARITIES), :] = (
                        yslab_s[q, pl.ds(r * HM + b, frames, stride=BATCH), :])
        for q in range(S5_SLABS):
            yb_s[:, q * LANES:(q + 1) * LANES] = ybm_s[q].astype(bf16)

    def gmlp(g):
        cs = slice(g * GMLP_GROUP_DIM, (g + 1) * GMLP_GROUP_DIM)
        vcat = jnp.concatenate([v_s[b * TT:(b + 1) * TT, cs] for b in range(BATCH)], axis=1)
        mixed = jnp.dot(ws_ref[g], vcat, preferred_element_type=f32)
        for b in range(BATCH):
            rs = slice(b * TT, (b + 1) * TT)
            ya_s[rs, cs] = (ug_s[rs, cs] * (mixed[:, b * LANES:(b + 1) * LANES] + bs_ref[g])).astype(bf16)

    def merge_chunk(c, cn):
        gb = jnp.dot(h_s[...], win_ref[:, s4 + c:s4 + c + cn], preferred_element_type=f32)
        ma = jnp.dot(ya_s[...], pa_ref[:, c:c + cn], preferred_element_type=f32)
        mb = jnp.dot(yb_s[...], pb_ref[:, c:c + cn], preferred_element_type=f32)
        mg_s[:, c:c + cn] = (sga_s[:, c:c + cn] * ma + _twice_sigmoid_of_twice(gb) * mb).astype(bf16)

    def out_chunk(c, cn):
        res = x_ref[:, :, c:c + cn].reshape(TM, cn) + jnp.dot(mg_s[...], wout_ref[:, c:c + cn],
                                                              preferred_element_type=f32)
        o_ref[:, :, c:c + cn] = res.reshape(BATCH, TT, cn)

    @pl.when(step >= PREP_STEPS)
    def _():
        rmsnorm()
        xb_proj()
        u_chunk(0 * cn)
        bu(0)
        bu(1)
        u_chunk(1 * cn)
        bu(2)
        scan_part(0, 0)
        bu(3)
        scan_part(0, 1)
        u_chunk(2 * cn)
        scan_part(0, 2)
        ga_chunk(0 * cn)
        u_chunk(3 * cn)
        scan_part(0, 3)
        cy(0)
        scan_part(1, 0)
        v_chunk(0 * cn)
        ga_chunk(1 * cn)
        scan_part(1, 1)
        v_chunk(1 * cn)
        ga_chunk(2 * cn)
        scan_part(1, 2)
        v_chunk(2 * cn)
        ga_chunk(3 * cn)
        scan_part(1, 3)
        v_chunk(3 * cn)
        cy(1)
        layer_norm()
        for q in range(S5_SLABS):
            for r in range(S5_PARITIES):
                uext_s[q, r, 0:SUBLANES, :] = uext_s[q, r, HM:HM + SUBLANES, :]
        glu()
        for g in range(GMLP_GROUPS):
            gmlp(g)
        for c in range(0, D_MODEL, 2 * cn):
            merge_chunk(c, 2 * cn)
        for c in range(0, D_MODEL, 2 * cn):
            out_chunk(c, 2 * cn)

        wg_o[...] = (0.5 * wg_ref[...]).astype(bf16)
        wu_o[...] = wu_ref[...].astype(bf16)
        wd_o[...] = wd_ref[...].astype(bf16)


def _ffn_kernel(x_ref, g2_ref, wg_ref, wu_ref, wd_ref, gf_ref, o_ref, h_s, hid_s, *, final_norm):
    bf16 = jnp.bfloat16
    x = x_ref[...]
    ms = jnp.mean(x * x, axis=-1, keepdims=True)
    h_s[...] = (x * lax.rsqrt(ms + EPS) * g2_ref[...]).astype(bf16)
    cn = 256
    for c in range(0, FFN_HIDDEN, cn):
        gate = jnp.dot(h_s[...], wg_ref[:, c:c + cn], preferred_element_type=jnp.float32)
        up = jnp.dot(h_s[...], wu_ref[:, c:c + cn], preferred_element_type=jnp.float32)
        hid_s[:, c:c + cn] = (gate * _twice_sigmoid_of_twice(gate) * up).astype(bf16)
    y = x_ref[...] + jnp.dot(hid_s[...], wd_ref[...], preferred_element_type=jnp.float32)
    if final_norm:
        ms2 = jnp.mean(y * y, axis=-1, keepdims=True)
        y = y * lax.rsqrt(ms2 + EPS) * gf_ref[...]
    o_ref[...] = y


def _resident(shape):
    nd = len(shape)
    return pl.BlockSpec(shape, lambda *_: (0,) * nd, pipeline_mode=pl.Buffered(1))


def _cmul(ar, ai, br, bi):
    return ar * br - ai * bi, ar * bi + ai * br


def _s5_params(lam_re, lam_im, log_dt, b_re, b_im, c_re, c_im):
    f32 = jnp.float32
    bf16 = jnp.bfloat16
    p, h = S5_STATE, S5_GROUP_DIM
    hg = S5_HALF_GROUPS
    dt = jnp.exp(log_dt.astype(f32))[:, None]
    lr = lam_re.astype(f32)
    li = lam_im.astype(f32)
    mag = jnp.exp(lr * dt)
    ab_re = mag * jnp.cos(li * dt)
    ab_im = mag * jnp.sin(li * dt)
    den = lr * lr + li * li
    nr = ab_re - 1.0
    coef_re = (nr * lr + ab_im * li) / den
    coef_im = (ab_im * lr - nr * li) / den
    br = b_re.astype(f32)
    bi = b_im.astype(f32)
    bb_re = coef_re[..., None] * br - coef_im[..., None] * bi
    bb_im = coef_re[..., None] * bi + coef_im[..., None] * br
    cr = c_re.astype(f32)
    ci = c_im.astype(f32)

    lag_re, lag_im = [bb_re], [bb_im]
    for _ in range(S5_LAGS - 1):
        nre, nim = _cmul(ab_re[..., None], ab_im[..., None], lag_re[-1], lag_im[-1])
        lag_re.append(nre)
        lag_im.append(nim)
    lag = jnp.stack([jnp.stack(lag_re, axis=1), jnp.stack(lag_im, axis=1)], axis=2)
    lag = lag.reshape(S5_SLABS, S5_HALVES, hg, S5_LAGS, 2, p, h)
    bc = jnp.transpose(lag, (0, 1, 3, 2, 6, 4, 5)).reshape(S5_SLABS, S5_HALVES, S5_LAGS * hg * h, 2 * p).astype(bf16)

    a2_re, a2_im = _cmul(ab_re, ab_im, ab_re, ab_im)
    a4_re, a4_im = _cmul(a2_re, a2_im, a2_re, a2_im)
    bshape = (S5_SLABS // 2, SUBLANES, S5_PAIR_COLS // 2)
    a4r = jnp.broadcast_to(a4_re.reshape(S5_SLABS // 2, 1, S5_PAIR_COLS // 2), bshape)
    a4i = jnp.broadcast_to(a4_im.reshape(S5_SLABS // 2, 1, S5_PAIR_COLS // 2), bshape)

    ca_re, ca_im = _cmul(cr, ci, ab_re[:, None, :], ab_im[:, None, :])
    cmap = jnp.stack([jnp.stack([cr, ca_re], axis=1), jnp.stack([-ci, -ca_im], axis=1)], axis=1)
    cmap = cmap.reshape(S5_SLABS, S5_HALVES, hg, 2, 2, h, p)
    cc = jnp.transpose(cmap, (0, 1, 3, 2, 6, 4, 5)).reshape(S5_SLABS, 2 * S5_SLAB_STATES, 2 * h)
    cc = jnp.pad(cc, ((0, 0), (0, 0), (0, LANES - 2 * h))).astype(bf16)

    k0 = jnp.einsum('ghp,gpi->gih', cr, bb_re) - jnp.einsum('ghp,gpi->gih', ci, bb_im)
    kc = jnp.pad(k0.reshape(S5_SLABS // 2, 2 * LANES, h), ((0, 0), (0, 0), (0, LANES - h))).astype(bf16)
    return bc, cc, kc, a4r, a4i


def _tilers():
    p, h, sg = S5_STATE, S5_GROUP_DIM, S5_SLAB_GROUPS
    tw = np.zeros((2, p, 2, S5_HALF_GROUPS, p), np.float32)
    tw[np.arange(2)[:, None], np.arange(p)[None, :], np.arange(2)[:, None], :, np.arange(p)[None, :]] = 1.0
    tc = np.zeros((LANES, 2, sg, h), np.float32)
    for v in range(2):
        tc[v * h + np.arange(h), v, :, np.arange(h)] = 1.0
    tk = np.zeros((LANES, 2 * sg, h), np.float32)
    tk[np.arange(h), :, np.arange(h)] = 1.0
    as_bf16 = lambda t, cols: jnp.asarray(t.reshape(-1, cols), jnp.bfloat16)
    return as_bf16(tw, 2 * S5_HALF_STATES), as_bf16(tc, 2 * LANES), as_bf16(tk, 2 * LANES)


def _mixer(x, g1, lng, lnb, ws, bs, bc, cc, kc, a4r, a4i, d, bglu, own_w, ffn_w, layer):
    f32 = jnp.float32
    bf16 = jnp.bfloat16
    weights = (g1, lng, lnb, ws, bs, bc, cc, kc) + _tilers() + (a4r, a4i, d, bglu)
    n_tiles = SEQ // TT
    tile = lambda i: jnp.maximum(i - PREP_STEPS, 0)
    prep = lambda i: jnp.minimum(i, PREP_STEPS - 1)
    x_spec = pl.BlockSpec((BATCH, TT, D_MODEL), lambda i: (0, tile(i), 0))
    own_specs = [pl.BlockSpec((None, w.shape[1] // PREP_STEPS, w.shape[2]), lambda i: (layer, prep(i), 0))
                 for w in own_w]
    own_scratch = [pltpu.VMEM(w.shape[1:], bf16) for w in own_w]
    assert all(w.shape[1] % (PREP_STEPS * BF16_ROWS) == 0 for w in own_w)
    gu_rows = D_MODEL // n_tiles
    wd_rows = BF16_ROWS * (FFN_HIDDEN // (2 * LANES))
    wd_steps = n_tiles // (FFN_HIDDEN // wd_rows)
    assert gu_rows % BF16_ROWS == 0 and FFN_HIDDEN % wd_rows == 0 and wd_steps * (FFN_HIDDEN // wd_rows) == n_tiles
    gu_in = pl.BlockSpec((None, gu_rows, FFN_HIDDEN), lambda i: (layer, tile(i), 0))
    wd_in = pl.BlockSpec((None, wd_rows, D_MODEL), lambda i: (layer, tile(i) // wd_steps, 0))
    gu_out = pl.BlockSpec((gu_rows, FFN_HIDDEN), lambda i: (tile(i), 0))
    wd_out = pl.BlockSpec((wd_rows, D_MODEL), lambda i: (tile(i) // wd_steps, 0))
    gu_shape = jax.ShapeDtypeStruct((D_MODEL, FFN_HIDDEN), bf16)
    wd_shape = jax.ShapeDtypeStruct((FFN_HIDDEN, D_MODEL), bf16)
    return pl.pallas_call(
        _mixer_kernel,
        grid=(PREP_STEPS + n_tiles,),
        in_specs=[x_spec] + [_resident(w.shape) for w in weights] + own_specs + [gu_in, gu_in, wd_in],
        out_specs=[x_spec, gu_out, gu_out, wd_out],
        out_shape=[jax.ShapeDtypeStruct((BATCH, SEQ, D_MODEL), f32), gu_shape, gu_shape, wd_shape],
        scratch_shapes=own_scratch + [
            pltpu.VMEM((S5_SLABS, S5_HALVES, 2 * LANES, 2 * S5_HALF_STATES), bf16),
            pltpu.VMEM((S5_SLABS, 2 * S5_SLAB_STATES, 2 * LANES), bf16),
            pltpu.VMEM((S5_SLABS // 2, 2 * LANES, 2 * LANES), bf16),
            pltpu.VMEM((TM, D_MODEL), bf16),
            pltpu.VMEM((TM, GMLP_WIDTH), f32),
            pltpu.VMEM((TM, GMLP_WIDTH), f32),
            pltpu.VMEM((TM, GMLP_WIDTH), bf16),
            pltpu.VMEM((TM, GMLP_WIDTH), bf16),
            pltpu.VMEM((S5_SLABS, TM, LANES), f32),
            pltpu.VMEM((S5_SLABS, S5_PARITIES, HM + SUBLANES, LANES), f32),
            pltpu.VMEM((S5_SLABS // 2, HM, S5_PAIR_COLS), f32),
            pltpu.VMEM((S5_SLABS // 2, HM, S5_PAIR_COLS), bf16),
            pltpu.VMEM((S5_SLABS // 2, SUBLANES, S5_PAIR_COLS), f32),
            pltpu.VMEM((TM, S5_WIDTH), f32),
            pltpu.VMEM((S5_SLABS, TM, LANES), f32),
            pltpu.VMEM((S5_SLABS, TM, LANES), f32),
            pltpu.VMEM((TM, S5_WIDTH), bf16),
            pltpu.VMEM((TM, D_MODEL), f32),
            pltpu.VMEM((TM, D_MODEL), bf16),
        ],
        compiler_params=pltpu.CompilerParams(
            dimension_semantics=("arbitrary",), vmem_limit_bytes=VMEM_LIMIT),
        name="mixer",
    )(x, *weights, *own_w, *ffn_w)


def _ffn(x, g2, wg, wu, wd, gf, final_norm):
    f32 = jnp.float32
    n = x.shape[0]
    weights = (g2, wg, wu, wd, gf)
    x_spec = pl.BlockSpec((FFN_TM, D_MODEL), lambda i: (i, 0))
    return pl.pallas_call(
        functools.partial(_ffn_kernel, final_norm=final_norm),
        grid=(n // FFN_TM,),
        in_specs=[x_spec] + [_resident(w.shape) for w in weights],
        out_specs=x_spec,
        out_shape=jax.ShapeDtypeStruct((n, D_MODEL), f32),
        scratch_shapes=[
            pltpu.VMEM((FFN_TM, D_MODEL), jnp.bfloat16),
            pltpu.VMEM((FFN_TM, FFN_HIDDEN), jnp.bfloat16),
        ],
        compiler_params=pltpu.CompilerParams(
            dimension_semantics=("arbitrary",), vmem_limit_bytes=VMEM_LIMIT),
        name="ffn",
    )(x, *weights)


def kernel(x, norm1_g, w_in, gmlp_ln_g, gmlp_ln_b, gmlp_ws, gmlp_bs, s5_lambda_re, s5_lambda_im, s5_log_dt, s5_b_re, s5_b_im, s5_c_re, s5_c_im, s5_d, s5_w_glu, s5_b_glu, w_branch_a, w_branch_b, w_out, norm2_g, w_ffn_gate, w_ffn_up, w_ffn_down, norm_f_g):
    f32 = jnp.float32
    bf16 = jnp.bfloat16
    depth = norm1_g.shape[0]
    cidx = jnp.arange(GMLP_BLOCK) // CHUNK
    mask = cidx[None, :] <= cidx[:, None]
    row = lambda v: v.astype(f32).reshape(1, -1)
    for l in range(depth):
        ws = jnp.where(mask[None], gmlp_ws[l], jnp.zeros_like(gmlp_ws[l])).astype(bf16)
        bs = jnp.broadcast_to(gmlp_bs[l].astype(f32)[:, :, None], (GMLP_GROUPS, GMLP_BLOCK, LANES))
        bc, cc, kc, a4r, a4i = _s5_params(s5_lambda_re[l], s5_lambda_im[l], s5_log_dt[l],
                                          s5_b_re[l], s5_b_im[l], s5_c_re[l], s5_c_im[l])
        own_w = (w_in, s5_w_glu, w_branch_a, w_branch_b, w_out)
        ffn_w = (w_ffn_gate, w_ffn_up, w_ffn_down)
        x, wg, wu, wd = _mixer(x, row(norm1_g[l]), row(gmlp_ln_g[l]), row(gmlp_ln_b[l]), ws, bs,
                               bc, cc, kc, a4r, a4i, row(s5_d[l]), row(0.5 * s5_b_glu[l]), own_w, ffn_w, l)
        y = _ffn(x.reshape(BATCH * SEQ, D_MODEL), row(norm2_g[l]), wg, wu, wd, row(norm_f_g), l == depth - 1)
        x = y.reshape(BATCH, SEQ, D_MODEL)
    return x
```

```python
import functools

import jax
import jax.numpy as jnp
import numpy as np
from jax import lax
from jax.experimental import pallas as pl
from jax.experimental.pallas import tpu as pltpu

D_MODEL = 1024
BATCH = 4
SEQ = 8192
CHUNK = 64
GMLP_BLOCK = 128
GMLP_GROUP_DIM = 128
GMLP_WIDTH = D_MODEL
GMLP_GROUPS = GMLP_WIDTH // GMLP_GROUP_DIM
S5_GROUP_DIM = 16
S5_WIDTH = D_MODEL // 2
S5_STATE = 64
FFN_HIDDEN = -(-8 * D_MODEL // 768) * 256
EPS = 1e-6

LANES = 128
SUBLANES = 8
BF16_ROWS = 2 * SUBLANES
S5_SLABS = S5_WIDTH // LANES
S5_SLAB_GROUPS = LANES // S5_GROUP_DIM
S5_SLAB_STATES = S5_SLAB_GROUPS * S5_STATE
S5_PAIR_COLS = 2 * 2 * S5_SLAB_STATES
S5_LAGS = 4
S5_HALVES = 2
S5_HALF_GROUPS = S5_SLAB_GROUPS // S5_HALVES
S5_HALF_STATES = S5_HALF_GROUPS * S5_STATE
S5_PARITIES = 2
TT = GMLP_BLOCK
TM = BATCH * TT
HM = TM // S5_PARITIES
ROWS_PER_STEP = 2 * SUBLANES
FFN_TM = 1024
FFN_SUBTILES = 4
PREP_STEPS = 8
VMEM_LIMIT = 60 * 1024 * 1024

_GELU_C = float(np.sqrt(2.0 / np.pi))
_LOG2_GROUP_DIM = S5_GROUP_DIM.bit_length() - 1
_LOG2_STATE = S5_STATE.bit_length() - 1
assert 1 << _LOG2_GROUP_DIM == S5_GROUP_DIM and 1 << _LOG2_STATE == S5_STATE


def _gelu(x):
    hx = 0.5 * x
    return hx * jnp.tanh(x * ((x * x) * (_GELU_C * 0.044715) + _GELU_C)) + hx


def _twice_sigmoid_of_twice(x_half):
    return jnp.tanh(x_half) + 1.0


def _mixer_kernel(x_ref, g1_ref, lng_ref, lnb_ref, ws_ref, bs_ref,
                  bc_ref, cc_ref, kc_ref, tw_ref, tc_ref, tk_ref, a4r_ref, a4i_ref, d_ref, bglu_ref,
                  win_f, wglu_f, pa_f, pb_f, wout_f, wg_ref, wu_ref, wd_ref,
                  o_ref, wg_o, wu_o, wd_o,
                  win_ref, wglu_ref, pa_ref, pb_ref, wout_ref, w4_s, cw_s, k0_s, h_s, ug_s, vg_s, v_s, ya_s, xbs_s, uext_s, st_s, xs_s, hst_s, ypre_s, yslab_s, ybm_s, yb_s,
                  sga_s, mg_s):
    f32 = jnp.float32
    bf16 = jnp.bfloat16
    step = pl.program_id(0)

    def block_diag(compact, tiler, row_group, col_group):
        full = jnp.dot(compact, tiler, preferred_element_type=f32)
        rg = row_group(lax.broadcasted_iota(jnp.int32, full.shape, 0))
        cg = col_group(lax.broadcasted_iota(jnp.int32, full.shape, 1))
        return jnp.where(rg == cg, full, 0.0).astype(bf16)

    s1 = GMLP_WIDTH
    s2 = 2 * GMLP_WIDTH
    s3 = s2 + S5_WIDTH
    s4 = s3 + D_MODEL

    @pl.when(step < PREP_STEPS)
    def _():
        def rows(ref):
            n = ref.shape[0]
            return pl.ds(pl.multiple_of(step * n, n), n)
        gate_col = lax.broadcasted_iota(jnp.int32, win_f.shape, 1) >= s3
        win_ref[rows(win_f), :] = (win_f[...] * jnp.where(gate_col, 0.5, 1.0)).astype(bf16)
        wglu_ref[rows(wglu_f), :] = (0.5 * wglu_f[...]).astype(bf16)
        pa_ref[rows(pa_f), :] = pa_f[...].astype(bf16)
        pb_ref[rows(pb_f), :] = (0.5 * pb_f[...]).astype(bf16)
        wout_ref[rows(wout_f), :] = (0.5 * wout_f[...]).astype(bf16)

    @pl.when(step == PREP_STEPS)
    def _():
        hst_s[...] = jnp.zeros_like(hst_s)
        uext_s[:, :, 0:SUBLANES, :] = jnp.zeros((S5_SLABS, S5_PARITIES, SUBLANES, LANES), f32)
        sg, hg = S5_SLAB_GROUPS, S5_HALF_GROUPS
        lg_half = S5_HALF_STATES.bit_length() - 1
        for q in range(S5_SLABS):
            for hf in range(S5_HALVES):
                w4_s[q, hf] = block_diag(bc_ref[q, hf], tw_ref[...],
                                         lambda r: (r >> _LOG2_GROUP_DIM) & (hg - 1),
                                         lambda c: (c >> _LOG2_STATE) & (hg - 1))
            cw_s[q] = block_diag(cc_ref[q], tc_ref[...],
                                 lambda r: ((r >> (lg_half + 1)) * hg) + ((r >> _LOG2_STATE) & (hg - 1)),
                                 lambda c: (c >> _LOG2_GROUP_DIM) & (sg - 1))
        for m in range(S5_SLABS // 2):
            k0_s[m] = block_diag(kc_ref[m], tk_ref[...],
                                 lambda r: r >> _LOG2_GROUP_DIM, lambda c: c >> _LOG2_GROUP_DIM)

    cn = 256
    ns = S5_SLAB_STATES
    hs = S5_HALF_STATES
    n_parts = 4
    steps_per_part = HM // ROWS_PER_STEP // n_parts
    frames = TT // S5_PARITIES

    hb_n = BATCH // 2
    hrows = hb_n * TT

    def rows(hb):
        return slice(hb * hrows, (hb + 1) * hrows)

    def rmsnorm(hb):
        x = x_ref[hb * hb_n:(hb + 1) * hb_n].reshape(hrows, D_MODEL)
        ms = jnp.mean(x * x, axis=-1, keepdims=True)
        h_s[rows(hb), :] = (x * lax.rsqrt(ms + EPS) * g1_ref[...]).astype(bf16)

    def u_chunk(c, hb):
        ug_s[rows(hb), c:c + cn] = _gelu(jnp.dot(h_s[rows(hb), :], win_ref[:, c:c + cn], preferred_element_type=f32))

    def v_chunk(c, hb):
        vg_s[rows(hb), c:c + cn] = _gelu(
            jnp.dot(h_s[rows(hb), :], win_ref[:, s1 + c:s1 + c + cn], preferred_element_type=f32))

    def layer_norm(hb):
        vg = vg_s[rows(hb), :]
        mu = jnp.mean(vg, axis=-1, keepdims=True)
        vc = vg - mu
        var = jnp.mean(vc * vc, axis=-1, keepdims=True)
        v_s[rows(hb), :] = (vc * lax.rsqrt(var + EPS) * lng_ref[...] + lnb_ref[...]).astype(bf16)

    def xb_proj():
        xb = jnp.dot(h_s[...], win_ref[:, s2:s3], preferred_element_type=f32)
        for q in range(S5_SLABS):
            xbs_s[q] = xb[:, q * LANES:(q + 1) * LANES]
        for q in range(S5_SLABS):
            for r in range(S5_PARITIES):
                for b in range(BATCH):
                    uext_s[q, r, pl.ds(SUBLANES + b, frames, stride=BATCH), :] = (
                        xbs_s[q, pl.ds(b * TT + r, frames, stride=S5_PARITIES), :])

    def bu(q):
        m, ql = divmod(q, 2)
        lags = []
        for d in range(S5_LAGS):
            off = SUBLANES - BATCH * ((d + 1) // 2)
            lags.append(uext_s[q, d % 2, off:off + HM, :])
        low = lax.broadcasted_iota(jnp.int32, (HM, LANES), 1) < LANES // 2
        for hf in range(S5_HALVES):
            tiles = []
            for d in range(0, S5_LAGS, 2):
                if hf == 0:
                    t = jnp.where(low, lags[d], pltpu.roll(lags[d + 1], LANES // 2, 1))
                else:
                    t = jnp.where(low, pltpu.roll(lags[d], LANES // 2, 1), lags[d + 1])
                tiles.append(t.astype(bf16))
            lhs = jnp.concatenate(tiles, axis=1)
            c0 = ql * 2 * ns + hf * 2 * hs
            st_s[m, :, c0:c0 + 2 * hs] = jnp.dot(lhs, w4_s[q, hf], preferred_element_type=f32)

    def split(z):
        res = [z[:, k * 2 * hs:k * 2 * hs + hs] for k in range(2 * S5_HALVES)]
        ims = [z[:, k * 2 * hs + hs:(k + 1) * 2 * hs] for k in range(2 * S5_HALVES)]
        return jnp.concatenate(res, axis=1), jnp.concatenate(ims, axis=1)

    def merge(zr, zi):
        parts = []
        for k in range(2 * S5_HALVES):
            parts += [zr[:, k * hs:(k + 1) * hs], zi[:, k * hs:(k + 1) * hs]]
        return jnp.concatenate(parts, axis=1)

    carry = {}

    def scan_part(m, p):
        xr, xi = split(hst_s[m]) if p == 0 else carry[m]
        for i in range(p * steps_per_part, (p + 1) * steps_per_part):
            r0 = i * ROWS_PER_STEP
            outs = []
            for k in range(ROWS_PER_STEP // SUBLANES):
                br, bi = split(st_s[m, r0 + k * SUBLANES:r0 + (k + 1) * SUBLANES, :])
                a4r = a4r_ref[m]
                a4i = a4i_ref[m]
                xr, xi = a4r * xr - a4i * xi + br, a4r * xi + a4i * xr + bi
                outs.append(merge(xr, xi))
            xs_s[m, r0:r0 + ROWS_PER_STEP, :] = jnp.concatenate(outs, axis=0).astype(bf16)
        carry[m] = (xr, xi)
        if p == n_parts - 1:
            hst_s[m] = merge(xr, xi)

    def cur(q, r):
        return uext_s[q, r, SUBLANES:SUBLANES + HM, :]

    def cy(m):
        u_odd = jnp.concatenate([cur(2 * m, 1), cur(2 * m + 1, 1)], axis=1)
        direct = jnp.dot(u_odd.astype(bf16), k0_s[m], preferred_element_type=f32)
        for ql in range(2):
            q = 2 * m + ql
            cs = slice(q * LANES, (q + 1) * LANES)
            y2 = jnp.dot(xs_s[m, :, ql * 2 * ns:(ql + 1) * 2 * ns], cw_s[q], preferred_element_type=f32)
            ypre_s[0:HM, cs] = y2[:, 0:LANES] + d_ref[:, cs] * cur(q, 0)
            ypre_s[HM:TM, cs] = (y2[:, LANES:2 * LANES] + direct[:, ql * LANES:(ql + 1) * LANES]
                                 + d_ref[:, cs] * cur(q, 1))

    def ga_chunk(c, hb):
        sga_s[rows(hb), c:c + cn] = _twice_sigmoid_of_twice(
            jnp.dot(h_s[rows(hb), :], win_ref[:, s3 + c:s3 + c + cn], preferred_element_type=f32))

    def glu():
        yact = _gelu(ypre_s[...])
        z = jnp.dot(yact.astype(bf16), wglu_ref[...], preferred_element_type=f32) + bglu_ref[...]
        ybp = yact * _twice_sigmoid_of_twice(z)
        for q in range(S5_SLABS):
            yslab_s[q] = ybp[:, q * LANES:(q + 1) * LANES]
        for q in range(S5_SLABS):
            for r in range(S5_PARITIES):
                for b in range(BATCH):
                    ybm_s[q, pl.ds(b * TT + r, frames, stride=S5_PARITIES), :] = (
                        yslab_s[q, pl.ds(r * HM + b, frames, stride=BATCH), :])
        for q in range(S5_SLABS):
            yb_s[:, q * LANES:(q + 1) * LANES] = ybm_s[q].astype(bf16)

    def gmlp(g, hb):
        cs = slice(g * GMLP_GROUP_DIM, (g + 1) * GMLP_GROUP_DIM)
        bb = range(hb * hb_n, (hb + 1) * hb_n)
        vcat = jnp.concatenate([v_s[b * TT:(b + 1) * TT, cs] for b in bb], axis=1)
        mixed = jnp.dot(ws_ref[g], vcat, preferred_element_type=f32)
        for k, b in enumerate(bb):
            rs = slice(b * TT, (b + 1) * TT)
            ya_s[rs, cs] = (ug_s[rs, cs] * (mixed[:, k * LANES:(k + 1) * LANES] + bs_ref[g])).astype(bf16)

    def merge_chunk(c, cn, hb):
        rs = rows(hb)
        gb = jnp.dot(h_s[rs, :], win_ref[:, s4 + c:s4 + c + cn], preferred_element_type=f32)
        ma = jnp.dot(ya_s[rs, :], pa_ref[:, c:c + cn], preferred_element_type=f32)
        mb = jnp.dot(yb_s[rs, :], pb_ref[:, c:c + cn], preferred_element_type=f32)
        mg_s[rs, c:c + cn] = (sga_s[rs, c:c + cn] * ma + _twice_sigmoid_of_twice(gb) * mb).astype(bf16)

    def out_chunk(c, cn, hb):
        res = x_ref[hb * hb_n:(hb + 1) * hb_n, :, c:c + cn].reshape(hrows, cn) + jnp.dot(
            mg_s[rows(hb), :], wout_ref[:, c:c + cn], preferred_element_type=f32)
        o_ref[hb * hb_n:(hb + 1) * hb_n, :, c:c + cn] = res.reshape(hb_n, TT, cn)

    @pl.when(step >= PREP_STEPS)
    def _():
        rmsnorm(0)
        rmsnorm(1)
        xb_proj()
        for c in range(0, GMLP_WIDTH, cn):
            u_chunk(c, 0)
        for q in range(S5_SLABS):
            bu(q)
        for c in range(0, GMLP_WIDTH, cn):
            v_chunk(c, 0)
        for p in range(n_parts):
            scan_part(0, p)
        layer_norm(0)
        cy(0)
        for c in range(0, D_MODEL, cn):
            ga_chunk(c, 0)
        for c in range(0, GMLP_WIDTH, cn):
            u_chunk(c, 1)
        for p in range(n_parts):
            scan_part(1, p)
        for c in range(0, GMLP_WIDTH, cn):
            v_chunk(c, 1)
        layer_norm(1)
        cy(1)
        for c in range(0, D_MODEL, cn):
            ga_chunk(c, 1)
        for q in range(S5_SLABS):
            for r in range(S5_PARITIES):
                uext_s[q, r, 0:SUBLANES, :] = uext_s[q, r, HM:HM + SUBLANES, :]
        glu()
        for hb in range(2):
            for g in range(GMLP_GROUPS):
                gmlp(g, hb)
            for c in range(0, D_MODEL, 2 * cn):
                merge_chunk(c, 2 * cn, hb)
        for hb in range(2):
            for c in range(0, D_MODEL, 2 * cn):
                out_chunk(c, 2 * cn, hb)

        wg_o[...] = (0.5 * wg_ref[...]).astype(bf16)
        wu_o[...] = wu_ref[...].astype(bf16)
        wd_o[...] = wd_ref[...].astype(bf16)


def _ffn_kernel(x_ref, g2_ref, wg_ref, wu_ref, wd_ref, gf_ref, o_ref, h_s, hid_s, *, final_norm):
    bf16 = jnp.bfloat16
    cn = 256
    hr = FFN_TM // FFN_SUBTILES
    for r in range(0, FFN_TM, hr):
        rs = slice(r, r + hr)
        x = x_ref[rs, :]
        ms = jnp.mean(x * x, axis=-1, keepdims=True)
        h_s[rs, :] = (x * lax.rsqrt(ms + EPS) * g2_ref[...]).astype(bf16)
        for c in range(0, FFN_HIDDEN, cn):
            gate = jnp.dot(h_s[rs, :], wg_ref[:, c:c + cn], preferred_element_type=jnp.float32)
            up = jnp.dot(h_s[rs, :], wu_ref[:, c:c + cn], preferred_element_type=jnp.float32)
            hid_s[rs, c:c + cn] = (gate * _twice_sigmoid_of_twice(gate) * up).astype(bf16)
        y = x_ref[rs, :] + jnp.dot(hid_s[rs, :], wd_ref[...], preferred_element_type=jnp.float32)
        if final_norm:
            ms2 = jnp.mean(y * y, axis=-1, keepdims=True)
            y = y * lax.rsqrt(ms2 + EPS) * gf_ref[...]
        o_ref[rs, :] = y


def _resident(shape):
    nd = len(shape)
    return pl.BlockSpec(shape, lambda *_: (0,) * nd, pipeline_mode=pl.Buffered(1))


def _cmul(ar, ai, br, bi):
    return ar * br - ai * bi, ar * bi + ai * br


def _s5_params(lam_re, lam_im, log_dt, b_re, b_im, c_re, c_im):
    f32 = jnp.float32
    bf16 = jnp.bfloat16
    p, h = S5_STATE, S5_GROUP_DIM
    hg = S5_HALF_GROUPS
    dt = jnp.exp(log_dt.astype(f32))[:, None]
    lr = lam_re.astype(f32)
    li = lam_im.astype(f32)
    mag = jnp.exp(lr * dt)
    ab_re = mag * jnp.cos(li * dt)
    ab_im = mag * jnp.sin(li * dt)
    den = lr * lr + li * li
    nr = ab_re - 1.0
    coef_re = (nr * lr + ab_im * li) / den
    coef_im = (ab_im * lr - nr * li) / den
    br = b_re.astype(f32)
    bi = b_im.astype(f32)
    bb_re = coef_re[..., None] * br - coef_im[..., None] * bi
    bb_im = coef_re[..., None] * bi + coef_im[..., None] * br
    cr = c_re.astype(f32)
    ci = c_im.astype(f32)

    lag_re, lag_im = [bb_re], [bb_im]
    for _ in range(S5_LAGS - 1):
        nre, nim = _cmul(ab_re[..., None], ab_im[..., None], lag_re[-1], lag_im[-1])
        lag_re.append(nre)
        lag_im.append(nim)
    lag = jnp.stack([jnp.stack(lag_re, axis=1), jnp.stack(lag_im, axis=1)], axis=2)
    lag = lag.reshape(S5_SLABS, S5_HALVES, hg, S5_LAGS, 2, p, h)
    bc = jnp.transpose(lag, (0, 1, 3, 2, 6, 4, 5)).reshape(S5_SLABS, S5_HALVES, S5_LAGS * hg * h, 2 * p).astype(bf16)

    a2_re, a2_im = _cmul(ab_re, ab_im, ab_re, ab_im)
    a4_re, a4_im = _cmul(a2_re, a2_im, a2_re, a2_im)
    bshape = (S5_SLABS // 2, SUBLANES, S5_PAIR_COLS // 2)
    a4r = jnp.broadcast_to(a4_re.reshape(S5_SLABS // 2, 1, S5_PAIR_COLS // 2), bshape)
    a4i = jnp.broadcast_to(a4_im.reshape(S5_SLABS // 2, 1, S5_PAIR_COLS // 2), bshape)

    ca_re, ca_im = _cmul(cr, ci, ab_re[:, None, :], ab_im[:, None, :])
    cmap = jnp.stack([jnp.stack([cr, ca_re], axis=1), jnp.stack([-ci, -ca_im], axis=1)], axis=1)
    cmap = cmap.reshape(S5_SLABS, S5_HALVES, hg, 2, 2, h, p)
    cc = jnp.transpose(cmap, (0, 1, 3, 2, 6, 4, 5)).reshape(S5_SLABS, 2 * S5_SLAB_STATES, 2 * h)
    cc = jnp.pad(cc, ((0, 0), (0, 0), (0, LANES - 2 * h))).astype(bf16)

    k0 = jnp.einsum('ghp,gpi->gih', cr, bb_re) - jnp.einsum('ghp,gpi->gih', ci, bb_im)
    kc = jnp.pad(k0.reshape(S5_SLABS // 2, 2 * LANES, h), ((0, 0), (0, 0), (0, LANES - h))).astype(bf16)
    return bc, cc, kc, a4r, a4i


def _tilers():
    p, h, sg = S5_STATE, S5_GROUP_DIM, S5_SLAB_GROUPS
    tw = np.zeros((2, p, 2, S5_HALF_GROUPS, p), np.float32)
    tw[np.arange(2)[:, None], np.arange(p)[None, :], np.arange(2)[:, None], :, np.arange(p)[None, :]] = 1.0
    tc = np.zeros((LANES, 2, sg, h), np.float32)
    for v in range(2):
        tc[v * h + np.arange(h), v, :, np.arange(h)] = 1.0
    tk = np.zeros((LANES, 2 * sg, h), np.float32)
    tk[np.arange(h), :, np.arange(h)] = 1.0
    as_bf16 = lambda t, cols: jnp.asarray(t.reshape(-1, cols), jnp.bfloat16)
    return as_bf16(tw, 2 * S5_HALF_STATES), as_bf16(tc, 2 * LANES), as_bf16(tk, 2 * LANES)


def _mixer(x, g1, lng, lnb, ws, bs, bc, cc, kc, a4r, a4i, d, bglu, own_w, ffn_w, layer):
    f32 = jnp.float32
    bf16 = jnp.bfloat16
    weights = (g1, lng, lnb, ws, bs, bc, cc, kc) + _tilers() + (a4r, a4i, d, bglu)
    n_tiles = SEQ // TT
    tile = lambda i: jnp.maximum(i - PREP_STEPS, 0)
    prep = lambda i: jnp.minimum(i, PREP_STEPS - 1)
    x_spec = pl.BlockSpec((BATCH, TT, D_MODEL), lambda i: (0, tile(i), 0))
    own_specs = [pl.BlockSpec((None, w.shape[1] // PREP_STEPS, w.shape[2]), lambda i: (layer, prep(i), 0))
                 for w in own_w]
    own_scratch = [pltpu.VMEM(w.shape[1:], bf16) for w in own_w]
    assert all(w.shape[1] % (PREP_STEPS * BF16_ROWS) == 0 for w in own_w)
    gu_rows = D_MODEL // n_tiles
    wd_rows = BF16_ROWS * (FFN_HIDDEN // (2 * LANES))
    wd_steps = n_tiles // (FFN_HIDDEN // wd_rows)
    assert gu_rows % BF16_ROWS == 0 and FFN_HIDDEN % wd_rows == 0 and wd_steps * (FFN_HIDDEN // wd_rows) == n_tiles
    gu_in = pl.BlockSpec((None, gu_rows, FFN_HIDDEN), lambda i: (layer, tile(i), 0))
    wd_in = pl.BlockSpec((None, wd_rows, D_MODEL), lambda i: (layer, tile(i) // wd_steps, 0))
    gu_out = pl.BlockSpec((gu_rows, FFN_HIDDEN), lambda i: (tile(i), 0))
    wd_out = pl.BlockSpec((wd_rows, D_MODEL), lambda i: (tile(i) // wd_steps, 0))
    gu_shape = jax.ShapeDtypeStruct((D_MODEL, FFN_HIDDEN), bf16)
    wd_shape = jax.ShapeDtypeStruct((FFN_HIDDEN, D_MODEL), bf16)
    return pl.pallas_call(
        _mixer_kernel,
        grid=(PREP_STEPS + n_tiles,),
        in_specs=[x_spec] + [_resident(w.shape) for w in weights] + own_specs + [gu_in, gu_in, wd_in],
        out_specs=[x_spec, gu_out, gu_out, wd_out],
        out_shape=[jax.ShapeDtypeStruct((BATCH, SEQ, D_MODEL), f32), gu_shape, gu_shape, wd_shape],
        scratch_shapes=own_scratch + [
            pltpu.VMEM((S5_SLABS, S5_HALVES, 2 * LANES, 2 * S5_HALF_STATES), bf16),
            pltpu.VMEM((S5_SLABS, 2 * S5_SLAB_STATES, 2 * LANES), bf16),
            pltpu.VMEM((S5_SLABS // 2, 2 * LANES, 2 * LANES), bf16),
            pltpu.VMEM((TM, D_MODEL), bf16),
            pltpu.VMEM((TM, GMLP_WIDTH), f32),
            pltpu.VMEM((TM, GMLP_WIDTH), f32),
            pltpu.VMEM((TM, GMLP_WIDTH), bf16),
            pltpu.VMEM((TM, GMLP_WIDTH), bf16),
            pltpu.VMEM((S5_SLABS, TM, LANES), f32),
            pltpu.VMEM((S5_SLABS, S5_PARITIES, HM + SUBLANES, LANES), f32),
            pltpu.VMEM((S5_SLABS // 2, HM, S5_PAIR_COLS), f32),
            pltpu.VMEM((S5_SLABS // 2, HM, S5_PAIR_COLS), bf16),
            pltpu.VMEM((S5_SLABS // 2, SUBLANES, S5_PAIR_COLS), f32),
            pltpu.VMEM((TM, S5_WIDTH), f32),
            pltpu.VMEM((S5_SLABS, TM, LANES), f32),
            pltpu.VMEM((S5_SLABS, TM, LANES), f32),
            pltpu.VMEM((TM, S5_WIDTH), bf16),
            pltpu.VMEM((TM, D_MODEL), f32),
            pltpu.VMEM((TM, D_MODEL), bf16),
        ],
        compiler_params=pltpu.CompilerParams(
            dimension_semantics=("arbitrary",), vmem_limit_bytes=VMEM_LIMIT),
        name="mixer",
    )(x, *weights, *own_w, *ffn_w)


def _ffn(x, g2, wg, wu, wd, gf, final_norm):
    f32 = jnp.float32
    n = x.shape[0]
    weights = (g2, wg, wu, wd, gf)
    x_spec = pl.BlockSpec((FFN_TM, D_MODEL), lambda i: (i, 0))
    return pl.pallas_call(
        functools.partial(_ffn_kernel, final_norm=final_norm),
        grid=(n // FFN_TM,),
        in_specs=[x_spec] + [_resident(w.shape) for w in weights],
        out_specs=x_spec,
        out_shape=jax.ShapeDtypeStruct((n, D_MODEL), f32),
        scratch_shapes=[
            pltpu.VMEM((FFN_TM, D_MODEL), jnp.bfloat16),
            pltpu.VMEM((FFN_TM, FFN_HIDDEN), jnp.bfloat16),
        ],
        compiler_params=pltpu.CompilerParams(
            dimension_semantics=("arbitrary",), vmem_limit_bytes=VMEM_LIMIT),
        name="ffn",
    )(x, *weights)


def kernel(x, norm1_g, w_in, gmlp_ln_g, gmlp_ln_b, gmlp_ws, gmlp_bs, s5_lambda_re, s5_lambda_im, s5_log_dt, s5_b_re, s5_b_im, s5_c_re, s5_c_im, s5_d, s5_w_glu, s5_b_glu, w_branch_a, w_branch_b, w_out, norm2_g, w_ffn_gate, w_ffn_up, w_ffn_down, norm_f_g):
    f32 = jnp.float32
    bf16 = jnp.bfloat16
    depth = norm1_g.shape[0]
    cidx = jnp.arange(GMLP_BLOCK) // CHUNK
    mask = cidx[None, :] <= cidx[:, None]
    row = lambda v: v.astype(f32).reshape(1, -1)
    for l in range(depth):
        ws = jnp.where(mask[None], gmlp_ws[l], jnp.zeros_like(gmlp_ws[l])).astype(bf16)
        bs = jnp.broadcast_to(gmlp_bs[l].astype(f32)[:, :, None], (GMLP_GROUPS, GMLP_BLOCK, LANES))
        bc, cc, kc, a4r, a4i = _s5_params(s5_lambda_re[l], s5_lambda_im[l], s5_log_dt[l],
                                          s5_b_re[l], s5_b_im[l], s5_c_re[l], s5_c_im[l])
        own_w = (w_in, s5_w_glu, w_branch_a, w_branch_b, w_out)
        ffn_w = (w_ffn_gate, w_ffn_up, w_ffn_down)
        x, wg, wu, wd = _mixer(x, row(norm1_g[l]), row(gmlp_ln_g[l]), row(gmlp_ln_b[l]), ws, bs,
                               bc, cc, kc, a4r, a4i, row(s5_d[l]), row(0.5 * s5_b_glu[l]), own_w, ffn_w, l)
        y = _ffn(x.reshape(BATCH * SEQ, D_MODEL), row(norm2_g[l]), wg, wu, wd, row(norm_f_g), l == depth - 1)
        x = y.reshape(BATCH, SEQ, D_MODEL)
    return x
```

```python
import functools

import jax
import jax.numpy as jnp
import numpy as np
from jax import lax
from jax.experimental import pallas as pl
from jax.experimental.pallas import tpu as pltpu

D_MODEL = 1024
BATCH = 4
SEQ = 8192
CHUNK = 64
GMLP_BLOCK = 128
GMLP_GROUP_DIM = 128
GMLP_WIDTH = D_MODEL
GMLP_GROUPS = GMLP_WIDTH // GMLP_GROUP_DIM
S5_GROUP_DIM = 16
S5_WIDTH = D_MODEL // 2
S5_STATE = 64
FFN_HIDDEN = -(-8 * D_MODEL // 768) * 256
EPS = 1e-6

LANES = 128
SUBLANES = 8
BF16_ROWS = 2 * SUBLANES
S5_SLABS = S5_WIDTH // LANES
S5_SLAB_GROUPS = LANES // S5_GROUP_DIM
S5_SLAB_STATES = S5_SLAB_GROUPS * S5_STATE
S5_PAIR_COLS = 2 * 2 * S5_SLAB_STATES
S5_LAGS = 4
S5_HALVES = 2
S5_HALF_GROUPS = S5_SLAB_GROUPS // S5_HALVES
S5_HALF_STATES = S5_HALF_GROUPS * S5_STATE
S5_PARITIES = 2
TT = GMLP_BLOCK
TM = BATCH * TT
HM = TM // S5_PARITIES
ROWS_PER_STEP = 2 * SUBLANES
FFN_TM = 1024
FFN_SUBTILES = 4
PREP_STEPS = 8
VMEM_LIMIT = 60 * 1024 * 1024

_GELU_C = float(np.sqrt(2.0 / np.pi))
_LOG2_GROUP_DIM = S5_GROUP_DIM.bit_length() - 1
_LOG2_STATE = S5_STATE.bit_length() - 1
assert 1 << _LOG2_GROUP_DIM == S5_GROUP_DIM and 1 << _LOG2_STATE == S5_STATE


def _gelu(x):
    hx = 0.5 * x
    return hx * jnp.tanh(x * ((x * x) * (_GELU_C * 0.044715) + _GELU_C)) + hx


def _twice_sigmoid_of_twice(x_half):
    return jnp.tanh(x_half) + 1.0


def _mixer_kernel(x_ref, g1_ref, lng_ref, lnb_ref, ws_ref, bs_ref,
                  bc_ref, cc_ref, kc_ref, tw_ref, tc_ref, tk_ref, a4r_ref, a4i_ref, d_ref, bglu_ref,
                  win_f, wglu_f, pa_f, pb_f, wout_f, wg_ref, wu_ref, wd_ref,
                  o_ref, wg_o, wu_o, wd_o,
                  win_ref, wglu_ref, pa_ref, pb_ref, wout_ref, w4_s, cw_s, k0_s, h_s, ug_s, vg_s, v_s, ya_s, xbs_s, uext_s, st_s, xs_s, hst_s, ypre_s, yslab_s, ybm_s, yb_s,
                  sga_s, mg_s):
    f32 = jnp.float32
    bf16 = jnp.bfloat16
    step = pl.program_id(0)

    def block_diag(compact, tiler, row_group, col_group):
        full = jnp.dot(compact, tiler, preferred_element_type=f32)
        rg = row_group(lax.broadcasted_iota(jnp.int32, full.shape, 0))
        cg = col_group(lax.broadcasted_iota(jnp.int32, full.shape, 1))
        return jnp.where(rg == cg, full, 0.0).astype(bf16)

    s1 = GMLP_WIDTH
    s2 = 2 * GMLP_WIDTH
    s3 = s2 + S5_WIDTH
    s4 = s3 + D_MODEL

    @pl.when(step < PREP_STEPS)
    def _():
        def rows(ref):
            n = ref.shape[0]
            return pl.ds(pl.multiple_of(step * n, n), n)
        gate_col = lax.broadcasted_iota(jnp.int32, win_f.shape, 1) >= s3
        win_ref[rows(win_f), :] = (win_f[...] * jnp.where(gate_col, 0.5, 1.0)).astype(bf16)
        wglu_ref[rows(wglu_f), :] = (0.5 * wglu_f[...]).astype(bf16)
        pa_ref[rows(pa_f), :] = pa_f[...].astype(bf16)
        pb_ref[rows(pb_f), :] = (0.5 * pb_f[...]).astype(bf16)
        wout_ref[rows(wout_f), :] = (0.5 * wout_f[...]).astype(bf16)

    @pl.when(step == PREP_STEPS)
    def _():
        hst_s[...] = jnp.zeros_like(hst_s)
        uext_s[:, :, 0:SUBLANES, :] = jnp.zeros((S5_SLABS, S5_PARITIES, SUBLANES, LANES), f32)
        sg, hg = S5_SLAB_GROUPS, S5_HALF_GROUPS
        lg_half = S5_HALF_STATES.bit_length() - 1
        for q in range(S5_SLABS):
            for hf in range(S5_HALVES):
                w4_s[q, hf] = block_diag(bc_ref[q, hf], tw_ref[...],
                                         lambda r: (r >> _LOG2_GROUP_DIM) & (hg - 1),
                                         lambda c: (c >> _LOG2_STATE) & (hg - 1))
            cw_s[q] = block_diag(cc_ref[q], tc_ref[...],
                                 lambda r: ((r >> (lg_half + 1)) * hg) + ((r >> _LOG2_STATE) & (hg - 1)),
                                 lambda c: (c >> _LOG2_GROUP_DIM) & (sg - 1))
        for m in range(S5_SLABS // 2):
            k0_s[m] = block_diag(kc_ref[m], tk_ref[...],
                                 lambda r: r >> _LOG2_GROUP_DIM, lambda c: c >> _LOG2_GROUP_DIM)

    cn = 256
    ns = S5_SLAB_STATES
    hs = S5_HALF_STATES
    n_parts = 4
    steps_per_part = HM // ROWS_PER_STEP // n_parts
    frames = TT // S5_PARITIES

    hb_n = BATCH // 2
    hrows = hb_n * TT

    def rows(hb):
        return slice(hb * hrows, (hb + 1) * hrows)

    def rmsnorm(hb):
        x = x_ref[hb * hb_n:(hb + 1) * hb_n].reshape(hrows, D_MODEL)
        ms = jnp.mean(x * x, axis=-1, keepdims=True)
        h_s[rows(hb), :] = (x * lax.rsqrt(ms + EPS) * g1_ref[...]).astype(bf16)

    def u_chunk(c, hb):
        ug_s[rows(hb), c:c + cn] = _gelu(jnp.dot(h_s[rows(hb), :], win_ref[:, c:c + cn], preferred_element_type=f32))

    def v_chunk(c, hb):
        vg_s[rows(hb), c:c + cn] = _gelu(
            jnp.dot(h_s[rows(hb), :], win_ref[:, s1 + c:s1 + c + cn], preferred_element_type=f32))

    def layer_norm(hb):
        vg = vg_s[rows(hb), :]
        mu = jnp.mean(vg, axis=-1, keepdims=True)
        vc = vg - mu
        var = jnp.mean(vc * vc, axis=-1, keepdims=True)
        v_s[rows(hb), :] = (vc * lax.rsqrt(var + EPS) * lng_ref[...] + lnb_ref[...]).astype(bf16)

    def xb_proj():
        xb = jnp.dot(h_s[...], win_ref[:, s2:s3], preferred_element_type=f32)
        for q in range(S5_SLABS):
            xbs_s[q] = xb[:, q * LANES:(q + 1) * LANES]
        for q in range(S5_SLABS):
            for r in range(S5_PARITIES):
                for b in range(BATCH):
                    uext_s[q, r, pl.ds(SUBLANES + b, frames, stride=BATCH), :] = (
                        xbs_s[q, pl.ds(b * TT + r, frames, stride=S5_PARITIES), :])

    def bu(q):
        m, ql = divmod(q, 2)
        lags = []
        for d in range(S5_LAGS):
            off = SUBLANES - BATCH * ((d + 1) // 2)
            lags.append(uext_s[q, d % 2, off:off + HM, :])
        low = lax.broadcasted_iota(jnp.int32, (HM, LANES), 1) < LANES // 2
        for hf in range(S5_HALVES):
            tiles = []
            for d in range(0, S5_LAGS, 2):
                if hf == 0:
                    t = jnp.where(low, lags[d], pltpu.roll(lags[d + 1], LANES // 2, 1))
                else:
                    t = jnp.where(low, pltpu.roll(lags[d], LANES // 2, 1), lags[d + 1])
                tiles.append(t.astype(bf16))
            lhs = jnp.concatenate(tiles, axis=1)
            c0 = ql * 2 * ns + hf * 2 * hs
            st_s[m, :, c0:c0 + 2 * hs] = jnp.dot(lhs, w4_s[q, hf], preferred_element_type=f32)

    def split(z):
        res = [z[:, k * 2 * hs:k * 2 * hs + hs] for k in range(2 * S5_HALVES)]
        ims = [z[:, k * 2 * hs + hs:(k + 1) * 2 * hs] for k in range(2 * S5_HALVES)]
        return jnp.concatenate(res, axis=1), jnp.concatenate(ims, axis=1)

    def merge(zr, zi):
        parts = []
        for k in range(2 * S5_HALVES):
            parts += [zr[:, k * hs:(k + 1) * hs], zi[:, k * hs:(k + 1) * hs]]
        return jnp.concatenate(parts, axis=1)

    carry = {}

    def scan_part(m, p):
        xr, xi = split(hst_s[m]) if p == 0 else carry[m]
        for i in range(p * steps_per_part, (p + 1) * steps_per_part):
            r0 = i * ROWS_PER_STEP
            outs = []
            for k in range(ROWS_PER_STEP // SUBLANES):
                br, bi = split(st_s[m, r0 + k * SUBLANES:r0 + (k + 1) * SUBLANES, :])
                a4r = a4r_ref[m]
                a4i = a4i_ref[m]
                xr, xi = a4r * xr - a4i * xi + br, a4r * xi + a4i * xr + bi
                outs.append(merge(xr, xi))
            xs_s[m, r0:r0 + ROWS_PER_STEP, :] = jnp.concatenate(outs, axis=0).astype(bf16)
        carry[m] = (xr, xi)
        if p == n_parts - 1:
            hst_s[m] = merge(xr, xi)

    def cur(q, r):
        return uext_s[q, r, SUBLANES:SUBLANES + HM, :]

    def cy(m):
        u_odd = jnp.concatenate([cur(2 * m, 1), cur(2 * m + 1, 1)], axis=1)
        direct = jnp.dot(u_odd.astype(bf16), k0_s[m], preferred_element_type=f32)
        for ql in range(2):
            q = 2 * m + ql
            cs = slice(q * LANES, (q + 1) * LANES)
            y2 = jnp.dot(xs_s[m, :, ql * 2 * ns:(ql + 1) * 2 * ns], cw_s[q], preferred_element_type=f32)
            ypre_s[0:HM, cs] = y2[:, 0:LANES] + d_ref[:, cs] * cur(q, 0)
            ypre_s[HM:TM, cs] = (y2[:, LANES:2 * LANES] + direct[:, ql * LANES:(ql + 1) * LANES]
                                 + d_ref[:, cs] * cur(q, 1))

    def ga_chunk(c, hb):
        sga_s[rows(hb), c:c + cn] = _twice_sigmoid_of_twice(
            jnp.dot(h_s[rows(hb), :], win_ref[:, s3 + c:s3 + c + cn], preferred_element_type=f32))

    def glu():
        yact = _gelu(ypre_s[...])
        z = jnp.dot(yact.astype(bf16), wglu_ref[...], preferred_element_type=f32) + bglu_ref[...]
        ybp = yact * _twice_sigmoid_of_twice(z)
        for q in range(S5_SLABS):
            yslab_s[q] = ybp[:, q * LANES:(q + 1) * LANES]
        for q in range(S5_SLABS):
            for r in range(S5_PARITIES):
                for b in range(BATCH):
                    ybm_s[q, pl.ds(b * TT + r, frames, stride=S5_PARITIES), :] = (
                        yslab_s[q, pl.ds(r * HM + b, frames, stride=BATCH), :])
        for q in range(S5_SLABS):
            yb_s[:, q * LANES:(q + 1) * LANES] = ybm_s[q].astype(bf16)

    def gmlp(g, hb):
        cs = slice(g * GMLP_GROUP_DIM, (g + 1) * GMLP_GROUP_DIM)
        bb = range(hb * hb_n, (hb + 1) * hb_n)
        vcat = jnp.concatenate([v_s[b * TT:(b + 1) * TT, cs] for b in bb], axis=1)
        mixed = jnp.dot(ws_ref[g], vcat, preferred_element_type=f32)
        for k, b in enumerate(bb):
            rs = slice(b * TT, (b + 1) * TT)
            ya_s[rs, cs] = (ug_s[rs, cs] * (mixed[:, k * LANES:(k + 1) * LANES] + bs_ref[g])).astype(bf16)

    def merge_chunk(c, cn, hb):
        rs = rows(hb)
        gb = jnp.dot(h_s[rs, :], win_ref[:, s4 + c:s4 + c + cn], preferred_element_type=f32)
        ma = jnp.dot(ya_s[rs, :], pa_ref[:, c:c + cn], preferred_element_type=f32)
        mb = jnp.dot(yb_s[rs, :], pb_ref[:, c:c + cn], preferred_element_type=f32)
        mg_s[rs, c:c + cn] = (sga_s[rs, c:c + cn] * ma + _twice_sigmoid_of_twice(gb) * mb).astype(bf16)

    def out_chunk(c, cn, hb):
        res = x_ref[hb * hb_n:(hb + 1) * hb_n, :, c:c + cn].reshape(hrows, cn) + jnp.dot(
            mg_s[rows(hb), :], wout_ref[:, c:c + cn], preferred_element_type=f32)
        o_ref[hb * hb_n:(hb + 1) * hb_n, :, c:c + cn] = res.reshape(hb_n, TT, cn)

    @pl.when(step >= PREP_STEPS)
    def _():
        rmsnorm(0)
        for c in range(0, GMLP_WIDTH, cn):
            u_chunk(c, 0)
        for c in range(0, GMLP_WIDTH, cn):
            v_chunk(c, 0)
        rmsnorm(1)
        xb_proj()
        layer_norm(0)
        for c in range(0, D_MODEL, cn):
            ga_chunk(c, 0)
        for q in range(S5_SLABS):
            bu(q)
        for c in range(0, GMLP_WIDTH, cn):
            u_chunk(c, 1)
        for p in range(n_parts):
            scan_part(0, p)
        cy(0)
        for c in range(0, GMLP_WIDTH, cn):
            v_chunk(c, 1)
        for p in range(n_parts):
            scan_part(1, p)
        layer_norm(1)
        cy(1)
        for c in range(0, D_MODEL, cn):
            ga_chunk(c, 1)
        for q in range(S5_SLABS):
            for r in range(S5_PARITIES):
                uext_s[q, r, 0:SUBLANES, :] = uext_s[q, r, HM:HM + SUBLANES, :]
        glu()
        for hb in range(2):
            for g in range(GMLP_GROUPS):
                gmlp(g, hb)
            for c in range(0, D_MODEL, 2 * cn):
                merge_chunk(c, 2 * cn, hb)
        for hb in range(2):
            for c in range(0, D_MODEL, 2 * cn):
                out_chunk(c, 2 * cn, hb)

        wg_o[...] = (0.5 * wg_ref[...]).astype(bf16)
        wu_o[...] = wu_ref[...].astype(bf16)
        wd_o[...] = wd_ref[...].astype(bf16)


def _ffn_kernel(x_ref, g2_ref, wg_ref, wu_ref, wd_ref, gf_ref, o_ref, h_s, hid_s, *, final_norm):
    bf16 = jnp.bfloat16
    cn = 256
    hr = FFN_TM // FFN_SUBTILES
    for r in range(0, FFN_TM, hr):
        rs = slice(r, r + hr)
        x = x_ref[rs, :]
        ms = jnp.mean(x * x, axis=-1, keepdims=True)
        h_s[rs, :] = (x * lax.rsqrt(ms + EPS) * g2_ref[...]).astype(bf16)
        for c in range(0, FFN_HIDDEN, cn):
            gate = jnp.dot(h_s[rs, :], wg_ref[:, c:c + cn], preferred_element_type=jnp.float32)
            up = jnp.dot(h_s[rs, :], wu_ref[:, c:c + cn], preferred_element_type=jnp.float32)
            hid_s[rs, c:c + cn] = (gate * _twice_sigmoid_of_twice(gate) * up).astype(bf16)
        y = x_ref[rs, :] + jnp.dot(hid_s[rs, :], wd_ref[...], preferred_element_type=jnp.float32)
        if final_norm:
            ms2 = jnp.mean(y * y, axis=-1, keepdims=True)
            y = y * lax.rsqrt(ms2 + EPS) * gf_ref[...]
        o_ref[rs, :] = y


def _resident(shape):
    nd = len(shape)
    return pl.BlockSpec(shape, lambda *_: (0,) * nd, pipeline_mode=pl.Buffered(1))


def _cmul(ar, ai, br, bi):
    return ar * br - ai * bi, ar * bi + ai * br


def _s5_params(lam_re, lam_im, log_dt, b_re, b_im, c_re, c_im):
    f32 = jnp.float32
    bf16 = jnp.bfloat16
    p, h = S5_STATE, S5_GROUP_DIM
    hg = S5_HALF_GROUPS
    dt = jnp.exp(log_dt.astype(f32))[:, None]
    lr = lam_re.astype(f32)
    li = lam_im.astype(f32)
    mag = jnp.exp(lr * dt)
    ab_re = mag * jnp.cos(li * dt)
    ab_im = mag * jnp.sin(li * dt)
    den = lr * lr + li * li
    nr = ab_re - 1.0
    coef_re = (nr * lr + ab_im * li) / den
    coef_im = (ab_im * lr - nr * li) / den
    br = b_re.astype(f32)
    bi = b_im.astype(f32)
    bb_re = coef_re[..., None] * br - coef_im[..., None] * bi
    bb_im = coef_re[..., None] * bi + coef_im[..., None] * br
    cr = c_re.astype(f32)
    ci = c_im.astype(f32)

    lag_re, lag_im = [bb_re], [bb_im]
    for _ in range(S5_LAGS - 1):
        nre, nim = _cmul(ab_re[..., None], ab_im[..., None], lag_re[-1], lag_im[-1])
        lag_re.append(nre)
        lag_im.append(nim)
    lag = jnp.stack([jnp.stack(lag_re, axis=1), jnp.stack(lag_im, axis=1)], axis=2)
    lag = lag.reshape(S5_SLABS, S5_HALVES, hg, S5_LAGS, 2, p, h)
    bc = jnp.transpose(lag, (0, 1, 3, 2, 6, 4, 5)).reshape(S5_SLABS, S5_HALVES, S5_LAGS * hg * h, 2 * p).astype(bf16)

    a2_re, a2_im = _cmul(ab_re, ab_im, ab_re, ab_im)
    a4_re, a4_im = _cmul(a2_re, a2_im, a2_re, a2_im)
    bshape = (S5_SLABS // 2, SUBLANES, S5_PAIR_COLS // 2)
    a4r = jnp.broadcast_to(a4_re.reshape(S5_SLABS // 2, 1, S5_PAIR_COLS // 2), bshape)
    a4i = jnp.broadcast_to(a4_im.reshape(S5_SLABS // 2, 1, S5_PAIR_COLS // 2), bshape)

    ca_re, ca_im = _cmul(cr, ci, ab_re[:, None, :], ab_im[:, None, :])
    cmap = jnp.stack([jnp.stack([cr, ca_re], axis=1), jnp.stack([-ci, -ca_im], axis=1)], axis=1)
    cmap = cmap.reshape(S5_SLABS, S5_HALVES, hg, 2, 2, h, p)
    cc = jnp.transpose(cmap, (0, 1, 3, 2, 6, 4, 5)).reshape(S5_SLABS, 2 * S5_SLAB_STATES, 2 * h)
    cc = jnp.pad(cc, ((0, 0), (0, 0), (0, LANES - 2 * h))).astype(bf16)

    k0 = jnp.einsum('ghp,gpi->gih', cr, bb_re) - jnp.einsum('ghp,gpi->gih', ci, bb_im)
    kc = jnp.pad(k0.reshape(S5_SLABS // 2, 2 * LANES, h), ((0, 0), (0, 0), (0, LANES - h))).astype(bf16)
    return bc, cc, kc, a4r, a4i


def _tilers():
    p, h, sg = S5_STATE, S5_GROUP_DIM, S5_SLAB_GROUPS
    tw = np.zeros((2, p, 2, S5_HALF_GROUPS, p), np.float32)
    tw[np.arange(2)[:, None], np.arange(p)[None, :], np.arange(2)[:, None], :, np.arange(p)[None, :]] = 1.0
    tc = np.zeros((LANES, 2, sg, h), np.float32)
    for v in range(2):
        tc[v * h + np.arange(h), v, :, np.arange(h)] = 1.0
    tk = np.zeros((LANES, 2 * sg, h), np.float32)
    tk[np.arange(h), :, np.arange(h)] = 1.0
    as_bf16 = lambda t, cols: jnp.asarray(t.reshape(-1, cols), jnp.bfloat16)
    return as_bf16(tw, 2 * S5_HALF_STATES), as_bf16(tc, 2 * LANES), as_bf16(tk, 2 * LANES)


def _mixer(x, g1, lng, lnb, ws, bs, bc, cc, kc, a4r, a4i, d, bglu, own_w, ffn_w, layer):
    f32 = jnp.float32
    bf16 = jnp.bfloat16
    weights = (g1, lng, lnb, ws, bs, bc, cc, kc) + _tilers() + (a4r, a4i, d, bglu)
    n_tiles = SEQ // TT
    tile = lambda i: jnp.maximum(i - PREP_STEPS, 0)
    prep = lambda i: jnp.minimum(i, PREP_STEPS - 1)
    x_spec = pl.BlockSpec((BATCH, TT, D_MODEL), lambda i: (0, tile(i), 0))
    own_specs = [pl.BlockSpec((None, w.shape[1] // PREP_STEPS, w.shape[2]), lambda i: (layer, prep(i), 0))
                 for w in own_w]
    own_scratch = [pltpu.VMEM(w.shape[1:], bf16) for w in own_w]
    assert all(w.shape[1] % (PREP_STEPS * BF16_ROWS) == 0 for w in own_w)
    gu_rows = D_MODEL // n_tiles
    wd_rows = BF16_ROWS * (FFN_HIDDEN // (2 * LANES))
    wd_steps = n_tiles // (FFN_HIDDEN // wd_rows)
    assert gu_rows % BF16_ROWS == 0 and FFN_HIDDEN % wd_rows == 0 and wd_steps * (FFN_HIDDEN // wd_rows) == n_tiles
    gu_in = pl.BlockSpec((None, gu_rows, FFN_HIDDEN), lambda i: (layer, tile(i), 0))
    wd_in = pl.BlockSpec((None, wd_rows, D_MODEL), lambda i: (layer, tile(i) // wd_steps, 0))
    gu_out = pl.BlockSpec((gu_rows, FFN_HIDDEN), lambda i: (tile(i), 0))
    wd_out = pl.BlockSpec((wd_rows, D_MODEL), lambda i: (tile(i) // wd_steps, 0))
    gu_shape = jax.ShapeDtypeStruct((D_MODEL, FFN_HIDDEN), bf16)
    wd_shape = jax.ShapeDtypeStruct((FFN_HIDDEN, D_MODEL), bf16)
    return pl.pallas_call(
        _mixer_kernel,
        grid=(PREP_STEPS + n_tiles,),
        in_specs=[x_spec] + [_resident(w.shape) for w in weights] + own_specs + [gu_in, gu_in, wd_in],
        out_specs=[x_spec, gu_out, gu_out, wd_out],
        out_shape=[jax.ShapeDtypeStruct((BATCH, SEQ, D_MODEL), f32), gu_shape, gu_shape, wd_shape],
        scratch_shapes=own_scratch + [
            pltpu.VMEM((S5_SLABS, S5_HALVES, 2 * LANES, 2 * S5_HALF_STATES), bf16),
            pltpu.VMEM((S5_SLABS, 2 * S5_SLAB_STATES, 2 * LANES), bf16),
            pltpu.VMEM((S5_SLABS // 2, 2 * LANES, 2 * LANES), bf16),
            pltpu.VMEM((TM, D_MODEL), bf16),
            pltpu.VMEM((TM, GMLP_WIDTH), f32),
            pltpu.VMEM((TM, GMLP_WIDTH), f32),
            pltpu.VMEM((TM, GMLP_WIDTH), bf16),
            pltpu.VMEM((TM, GMLP_WIDTH), bf16),
            pltpu.VMEM((S5_SLABS, TM, LANES), f32),
            pltpu.VMEM((S5_SLABS, S5_PARITIES, HM + SUBLANES, LANES), f32),
            pltpu.VMEM((S5_SLABS // 2, HM, S5_PAIR_COLS), f32),
            pltpu.VMEM((S5_SLABS // 2, HM, S5_PAIR_COLS), bf16),
            pltpu.VMEM((S5_SLABS // 2, SUBLANES, S5_PAIR_COLS), f32),
            pltpu.VMEM((TM, S5_WIDTH), f32),
            pltpu.VMEM((S5_SLABS, TM, LANES), f32),
            pltpu.VMEM((S5_SLABS, TM, LANES), f32),
            pltpu.VMEM((TM, S5_WIDTH), bf16),
            pltpu.VMEM((TM, D_MODEL), f32),
            pltpu.VMEM((TM, D_MODEL), bf16),
        ],
        compiler_params=pltpu.CompilerParams(
            dimension_semantics=("arbitrary",), vmem_limit_bytes=VMEM_LIMIT),
        name="mixer",
    )(x, *weights, *own_w, *ffn_w)


def _ffn(x, g2, wg, wu, wd, gf, final_norm):
    f32 = jnp.float32
    n = x.shape[0]
    weights = (g2, wg, wu, wd, gf)
    x_spec = pl.BlockSpec((FFN_TM, D_MODEL), lambda i: (i, 0))
    return pl.pallas_call(
        functools.partial(_ffn_kernel, final_norm=final_norm),
        grid=(n // FFN_TM,),
        in_specs=[x_spec] + [_resident(w.shape) for w in weights],
        out_specs=x_spec,
        out_shape=jax.ShapeDtypeStruct((n, D_MODEL), f32),
        scratch_shapes=[
            pltpu.VMEM((FFN_TM, D_MODEL), jnp.bfloat16),
            pltpu.VMEM((FFN_TM, FFN_HIDDEN), jnp.bfloat16),
        ],
        compiler_params=pltpu.CompilerParams(
            dimension_semantics=("arbitrary",), vmem_limit_bytes=VMEM_LIMIT),
        name="ffn",
    )(x, *weights)


def kernel(x, norm1_g, w_in, gmlp_ln_g, gmlp_ln_b, gmlp_ws, gmlp_bs, s5_lambda_re, s5_lambda_im, s5_log_dt, s5_b_re, s5_b_im, s5_c_re, s5_c_im, s5_d, s5_w_glu, s5_b_glu, w_branch_a, w_branch_b, w_out, norm2_g, w_ffn_gate, w_ffn_up, w_ffn_down, norm_f_g):
    f32 = jnp.float32
    bf16 = jnp.bfloat16
    depth = norm1_g.shape[0]
    cidx = jnp.arange(GMLP_BLOCK) // CHUNK
    mask = cidx[None, :] <= cidx[:, None]
    row = lambda v: v.astype(f32).reshape(1, -1)
    for l in range(depth):
        ws = jnp.where(mask[None], gmlp_ws[l], jnp.zeros_like(gmlp_ws[l])).astype(bf16)
        bs = jnp.broadcast_to(gmlp_bs[l].astype(f32)[:, :, None], (GMLP_GROUPS, GMLP_BLOCK, LANES))
        bc, cc, kc, a4r, a4i = _s5_params(s5_lambda_re[l], s5_lambda_im[l], s5_log_dt[l],
                                          s5_b_re[l], s5_b_im[l], s5_c_re[l], s5_c_im[l])
        own_w = (w_in, s5_w_glu, w_branch_a, w_branch_b, w_out)
        ffn_w = (w_ffn_gate, w_ffn_up, w_ffn_down)
        x, wg, wu, wd = _mixer(x, row(norm1_g[l]), row(gmlp_ln_g[l]), row(gmlp_ln_b[l]), ws, bs,
                               bc, cc, kc, a4r, a4i, row(s5_d[l]), row(0.5 * s5_b_glu[l]), own_w, ffn_w, l)
        y = _ffn(x.reshape(BATCH * SEQ, D_MODEL), row(norm2_g[l]), wg, wu, wd, row(norm_f_g), l == depth - 1)
        x = y.reshape(BATCH, SEQ, D_MODEL)
    return x
```

```python
import functools

import jax
import jax.numpy as jnp
import numpy as np
from jax import lax
from jax.experimental import pallas as pl
from jax.experimental.pallas import tpu as pltpu

D_MODEL = 1024
BATCH = 4
SEQ = 8192
CHUNK = 64
GMLP_BLOCK = 128
GMLP_GROUP_DIM = 128
GMLP_WIDTH = D_MODEL
GMLP_GROUPS = GMLP_WIDTH // GMLP_GROUP_DIM
S5_GROUP_DIM = 16
S5_WIDTH = D_MODEL // 2
S5_STATE = 64
FFN_HIDDEN = -(-8 * D_MODEL // 768) * 256
EPS = 1e-6

LANES = 128
SUBLANES = 8
BF16_ROWS = 2 * SUBLANES
S5_SLABS = S5_WIDTH // LANES
S5_SLAB_GROUPS = LANES // S5_GROUP_DIM
S5_SLAB_STATES = S5_SLAB_GROUPS * S5_STATE
S5_PAIR_COLS = 2 * 2 * S5_SLAB_STATES
S5_LAGS = 4
S5_HALVES = 2
S5_HALF_GROUPS = S5_SLAB_GROUPS // S5_HALVES
S5_HALF_STATES = S5_HALF_GROUPS * S5_STATE
S5_PARITIES = 2
TT = GMLP_BLOCK
TM = BATCH * TT
HM = TM // S5_PARITIES
ROWS_PER_STEP = 2 * SUBLANES
FFN_TM = 1024
FFN_SUBTILES = 4
PREP_STEPS = 8
VMEM_LIMIT = 60 * 1024 * 1024

_GELU_C = float(np.sqrt(2.0 / np.pi))
_LOG2_GROUP_DIM = S5_GROUP_DIM.bit_length() - 1
_LOG2_STATE = S5_STATE.bit_length() - 1
assert 1 << _LOG2_GROUP_DIM == S5_GROUP_DIM and 1 << _LOG2_STATE == S5_STATE


def _gelu(x):
    hx = 0.5 * x
    return hx * jnp.tanh(x * ((x * x) * (_GELU_C * 0.044715) + _GELU_C)) + hx


def _twice_sigmoid_of_twice(x_half):
    return jnp.tanh(x_half) + 1.0


def _mixer_kernel(x_ref, g1_ref, lng_ref, lnb_ref, ws_ref, bs_ref,
                  bc_ref, cc_ref, kc_ref, tw_ref, tc_ref, tk_ref, a4r_ref, a4i_ref, d_ref, bglu_ref,
                  win_f, wglu_f, pa_f, pb_f, wout_f, wg_ref, wu_ref, wd_ref,
                  o_ref, wg_o, wu_o, wd_o,
                  win_ref, wglu_ref, pa_ref, pb_ref, wout_ref, w4_s, cw_s, k0_s, h_s, ug_s, vg_s, v_s, ya_s, xbs_s, uext_s, st_s, xs_s, hst_s, ypre_s, yslab_s, ybm_s, yb_s,
                  sga_s, mg_s):
    f32 = jnp.float32
    bf16 = jnp.bfloat16
    step = pl.program_id(0)

    def block_diag(compact, tiler, row_group, col_group):
        full = jnp.dot(compact, tiler, preferred_element_type=f32)
        rg = row_group(lax.broadcasted_iota(jnp.int32, full.shape, 0))
        cg = col_group(lax.broadcasted_iota(jnp.int32, full.shape, 1))
        return jnp.where(rg == cg, full, 0.0).astype(bf16)

    s1 = GMLP_WIDTH
    s2 = 2 * GMLP_WIDTH
    s3 = s2 + S5_WIDTH
    s4 = s3 + D_MODEL

    @pl.when(step < PREP_STEPS)
    def _():
        def rows(ref):
            n = ref.shape[0]
            return pl.ds(pl.multiple_of(step * n, n), n)
        gate_col = lax.broadcasted_iota(jnp.int32, win_f.shape, 1) >= s3
        win_ref[rows(win_f), :] = (win_f[...] * jnp.where(gate_col, 0.5, 1.0)).astype(bf16)
        wglu_ref[rows(wglu_f), :] = (0.5 * wglu_f[...]).astype(bf16)
        pa_ref[rows(pa_f), :] = pa_f[...].astype(bf16)
        pb_ref[rows(pb_f), :] = (0.5 * pb_f[...]).astype(bf16)
        wout_ref[rows(wout_f), :] = (0.5 * wout_f[...]).astype(bf16)

    @pl.when(step == PREP_STEPS)
    def _():
        hst_s[...] = jnp.zeros_like(hst_s)
        uext_s[:, :, 0:SUBLANES, :] = jnp.zeros((S5_SLABS, S5_PARITIES, SUBLANES, LANES), f32)
        sg, hg = S5_SLAB_GROUPS, S5_HALF_GROUPS
        lg_half = S5_HALF_STATES.bit_length() - 1
        for q in range(S5_SLABS):
            for hf in range(S5_HALVES):
                w4_s[q, hf] = block_diag(bc_ref[q, hf], tw_ref[...],
                                         lambda r: (r >> _LOG2_GROUP_DIM) & (hg - 1),
                                         lambda c: (c >> _LOG2_STATE) & (hg - 1))
            cw_s[q] = block_diag(cc_ref[q], tc_ref[...],
                                 lambda r: ((r >> (lg_half + 1)) * hg) + ((r >> _LOG2_STATE) & (hg - 1)),
                                 lambda c: (c >> _LOG2_GROUP_DIM) & (sg - 1))
        for m in range(S5_SLABS // 2):
            k0_s[m] = block_diag(kc_ref[m], tk_ref[...],
                                 lambda r: r >> _LOG2_GROUP_DIM, lambda c: c >> _LOG2_GROUP_DIM)

    cn = 256
    ns = S5_SLAB_STATES
    hs = S5_HALF_STATES
    n_parts = 4
    steps_per_part = HM // ROWS_PER_STEP // n_parts
    frames = TT // S5_PARITIES

    hb_n = BATCH // 2
    hrows = hb_n * TT

    def rows(hb):
        return slice(hb * hrows, (hb + 1) * hrows)

    def rmsnorm(hb):
        x = x_ref[hb * hb_n:(hb + 1) * hb_n].reshape(hrows, D_MODEL)
        ms = jnp.mean(x * x, axis=-1, keepdims=True)
        h_s[rows(hb), :] = (x * lax.rsqrt(ms + EPS) * g1_ref[...]).astype(bf16)

    def u_chunk(c, hb):
        ug_s[rows(hb), c:c + cn] = _gelu(jnp.dot(h_s[rows(hb), :], win_ref[:, c:c + cn], preferred_element_type=f32))

    def v_chunk(c, hb):
        vg_s[rows(hb), c:c + cn] = _gelu(
            jnp.dot(h_s[rows(hb), :], win_ref[:, s1 + c:s1 + c + cn], preferred_element_type=f32))

    def layer_norm(hb):
        vg = vg_s[rows(hb), :]
        mu = jnp.mean(vg, axis=-1, keepdims=True)
        vc = vg - mu
        var = jnp.mean(vc * vc, axis=-1, keepdims=True)
        v_s[rows(hb), :] = (vc * lax.rsqrt(var + EPS) * lng_ref[...] + lnb_ref[...]).astype(bf16)

    def xb_proj():
        xb = jnp.dot(h_s[...], win_ref[:, s2:s3], preferred_element_type=f32)
        for q in range(S5_SLABS):
            xbs_s[q] = xb[:, q * LANES:(q + 1) * LANES]
        for q in range(S5_SLABS):
            for r in range(S5_PARITIES):
                for b in range(BATCH):
                    uext_s[q, r, pl.ds(SUBLANES + b, frames, stride=BATCH), :] = (
                        xbs_s[q, pl.ds(b * TT + r, frames, stride=S5_PARITIES), :])

    def bu(q):
        m, ql = divmod(q, 2)
        lags = []
        for d in range(S5_LAGS):
            off = SUBLANES - BATCH * ((d + 1) // 2)
            lags.append(uext_s[q, d % 2, off:off + HM, :])
        low = lax.broadcasted_iota(jnp.int32, (HM, LANES), 1) < LANES // 2
        for hf in range(S5_HALVES):
            tiles = []
            for d in range(0, S5_LAGS, 2):
                if hf == 0:
                    t = jnp.where(low, lags[d], pltpu.roll(lags[d + 1], LANES // 2, 1))
                else:
                    t = jnp.where(low, pltpu.roll(lags[d], LANES // 2, 1), lags[d + 1])
                tiles.append(t.astype(bf16))
            lhs = jnp.concatenate(tiles, axis=1)
            c0 = ql * 2 * ns + hf * 2 * hs
            st_s[m, :, c0:c0 + 2 * hs] = jnp.dot(lhs, w4_s[q, hf], preferred_element_type=f32)

    def split(z):
        res = [z[:, k * 2 * hs:k * 2 * hs + hs] for k in range(2 * S5_HALVES)]
        ims = [z[:, k * 2 * hs + hs:(k + 1) * 2 * hs] for k in range(2 * S5_HALVES)]
        return jnp.concatenate(res, axis=1), jnp.concatenate(ims, axis=1)

    def merge(zr, zi):
        parts = []
        for k in range(2 * S5_HALVES):
            parts += [zr[:, k * hs:(k + 1) * hs], zi[:, k * hs:(k + 1) * hs]]
        return jnp.concatenate(parts, axis=1)

    carry = {}

    def scan_part(m, p):
        xr, xi = split(hst_s[m]) if p == 0 else carry[m]
        for i in range(p * steps_per_part, (p + 1) * steps_per_part):
            r0 = i * ROWS_PER_STEP
            outs = []
            for k in range(ROWS_PER_STEP // SUBLANES):
                br, bi = split(st_s[m, r0 + k * SUBLANES:r0 + (k + 1) * SUBLANES, :])
                a4r = a4r_ref[m]
                a4i = a4i_ref[m]
                xr, xi = a4r * xr - a4i * xi + br, a4r * xi + a4i * xr + bi
                outs.append(merge(xr, xi))
            xs_s[m, r0:r0 + ROWS_PER_STEP, :] = jnp.concatenate(outs, axis=0).astype(bf16)
        carry[m] = (xr, xi)
        if p == n_parts - 1:
            hst_s[m] = merge(xr, xi)

    def cur(q, r):
        return uext_s[q, r, SUBLANES:SUBLANES + HM, :]

    def cy(m):
        u_odd = jnp.concatenate([cur(2 * m, 1), cur(2 * m + 1, 1)], axis=1)
        direct = jnp.dot(u_odd.astype(bf16), k0_s[m], preferred_element_type=f32)
        for ql in range(2):
            q = 2 * m + ql
            cs = slice(q * LANES, (q + 1) * LANES)
            y2 = jnp.dot(xs_s[m, :, ql * 2 * ns:(ql + 1) * 2 * ns], cw_s[q], preferred_element_type=f32)
            ypre_s[0:HM, cs] = y2[:, 0:LANES] + d_ref[:, cs] * cur(q, 0)
            ypre_s[HM:TM, cs] = (y2[:, LANES:2 * LANES] + direct[:, ql * LANES:(ql + 1) * LANES]
                                 + d_ref[:, cs] * cur(q, 1))

    def ga_chunk(c, hb):
        sga_s[rows(hb), c:c + cn] = _twice_sigmoid_of_twice(
            jnp.dot(h_s[rows(hb), :], win_ref[:, s3 + c:s3 + c + cn], preferred_element_type=f32))

    def glu():
        yact = _gelu(ypre_s[...])
        z = jnp.dot(yact.astype(bf16), wglu_ref[...], preferred_element_type=f32) + bglu_ref[...]
        ybp = yact * _twice_sigmoid_of_twice(z)
        for q in range(S5_SLABS):
            yslab_s[q] = ybp[:, q * LANES:(q + 1) * LANES]
        for q in range(S5_SLABS):
            for r in range(S5_PARITIES):
                for b in range(BATCH):
                    ybm_s[q, pl.ds(b * TT + r, frames, stride=S5_PARITIES), :] = (
                        yslab_s[q, pl.ds(r * HM + b, frames, stride=BATCH), :])
        for q in range(S5_SLABS):
            yb_s[:, q * LANES:(q + 1) * LANES] = ybm_s[q].astype(bf16)

    def gmlp(g, hb):
        cs = slice(g * GMLP_GROUP_DIM, (g + 1) * GMLP_GROUP_DIM)
        bb = range(hb * hb_n, (hb + 1) * hb_n)
        vcat = jnp.concatenate([v_s[b * TT:(b + 1) * TT, cs] for b in bb], axis=1)
        mixed = jnp.dot(ws_ref[g], vcat, preferred_element_type=f32)
        for k, b in enumerate(bb):
            rs = slice(b * TT, (b + 1) * TT)
            ya_s[rs, cs] = (ug_s[rs, cs] * (mixed[:, k * LANES:(k + 1) * LANES] + bs_ref[g])).astype(bf16)

    def merge_chunk(c, cn, hb):
        rs = rows(hb)
        gb = jnp.dot(h_s[rs, :], win_ref[:, s4 + c:s4 + c + cn], preferred_element_type=f32)
        ma = jnp.dot(ya_s[rs, :], pa_ref[:, c:c + cn], preferred_element_type=f32)
        mb = jnp.dot(yb_s[rs, :], pb_ref[:, c:c + cn], preferred_element_type=f32)
        mg_s[rs, c:c + cn] = (sga_s[rs, c:c + cn] * ma + _twice_sigmoid_of_twice(gb) * mb).astype(bf16)

    def out_chunk(c, cn, hb):
        res = x_ref[hb * hb_n:(hb + 1) * hb_n, :, c:c + cn].reshape(hrows, cn) + jnp.dot(
            mg_s[rows(hb), :], wout_ref[:, c:c + cn], preferred_element_type=f32)
        o_ref[hb * hb_n:(hb + 1) * hb_n, :, c:c + cn] = res.reshape(hb_n, TT, cn)

    @pl.when(step >= PREP_STEPS)
    def _():
        rmsnorm(0)
        for c in range(0, GMLP_WIDTH, cn):
            u_chunk(c, 0)
        for c in range(0, GMLP_WIDTH, cn):
            v_chunk(c, 0)
        rmsnorm(1)
        xb_proj()
        layer_norm(0)
        for c in range(0, D_MODEL, cn):
            ga_chunk(c, 0)
        for q in range(S5_SLABS):
            bu(q)
        for c in range(0, GMLP_WIDTH, cn):
            v_chunk(c, 1)
        for p in range(n_parts):
            scan_part(0, p)
        cy(0)
        layer_norm(1)
        for c in range(0, GMLP_WIDTH, cn):
            u_chunk(c, 1)
        for p in range(n_parts):
            scan_part(1, p)
        cy(1)
        for c in range(0, D_MODEL, cn):
            ga_chunk(c, 1)
        for q in range(S5_SLABS):
            for r in range(S5_PARITIES):
                uext_s[q, r, 0:SUBLANES, :] = uext_s[q, r, HM:HM + SUBLANES, :]
        glu()
        for hb in range(2):
            for g in range(GMLP_GROUPS):
                gmlp(g, hb)
            for c in range(0, D_MODEL, 2 * cn):
                merge_chunk(c, 2 * cn, hb)
            for c in range(0, D_MODEL, 2 * cn):
                out_chunk(c, 2 * cn, hb)

        wg_o[...] = (0.5 * wg_ref[...]).astype(bf16)
        wu_o[...] = wu_ref[...].astype(bf16)
        wd_o[...] = wd_ref[...].astype(bf16)


def _ffn_kernel(x_ref, g2_ref, wg_ref, wu_ref, wd_ref, gf_ref, o_ref, h_s, hid_s, *, final_norm):
    bf16 = jnp.bfloat16
    cn = 256
    hr = FFN_TM // FFN_SUBTILES
    for r in range(0, FFN_TM, hr):
        rs = slice(r, r + hr)
        x = x_ref[rs, :]
        ms = jnp.mean(x * x, axis=-1, keepdims=True)
        h_s[rs, :] = (x * lax.rsqrt(ms + EPS) * g2_ref[...]).astype(bf16)
        for c in range(0, FFN_HIDDEN, cn):
            gate = jnp.dot(h_s[rs, :], wg_ref[:, c:c + cn], preferred_element_type=jnp.float32)
            up = jnp.dot(h_s[rs, :], wu_ref[:, c:c + cn], preferred_element_type=jnp.float32)
            hid_s[rs, c:c + cn] = (gate * _twice_sigmoid_of_twice(gate) * up).astype(bf16)
        y = x_ref[rs, :] + jnp.dot(hid_s[rs, :], wd_ref[...], preferred_element_type=jnp.float32)
        if final_norm:
            ms2 = jnp.mean(y * y, axis=-1, keepdims=True)
            y = y * lax.rsqrt(ms2 + EPS) * gf_ref[...]
        o_ref[rs, :] = y


def _resident(shape):
    nd = len(shape)
    return pl.BlockSpec(shape, lambda *_: (0,) * nd, pipeline_mode=pl.Buffered(1))


def _cmul(ar, ai, br, bi):
    return ar * br - ai * bi, ar * bi + ai * br


def _s5_params(lam_re, lam_im, log_dt, b_re, b_im, c_re, c_im):
    f32 = jnp.float32
    bf16 = jnp.bfloat16
    p, h = S5_STATE, S5_GROUP_DIM
    hg = S5_HALF_GROUPS
    dt = jnp.exp(log_dt.astype(f32))[:, None]
    lr = lam_re.astype(f32)
    li = lam_im.astype(f32)
    mag = jnp.exp(lr * dt)
    ab_re = mag * jnp.cos(li * dt)
    ab_im = mag * jnp.sin(li * dt)
    den = lr * lr + li * li
    nr = ab_re - 1.0
    coef_re = (nr * lr + ab_im * li) / den
    coef_im = (ab_im * lr - nr * li) / den
    br = b_re.astype(f32)
    bi = b_im.astype(f32)
    bb_re = coef_re[..., None] * br - coef_im[..., None] * bi
    bb_im = coef_re[..., None] * bi + coef_im[..., None] * br
    cr = c_re.astype(f32)
    ci = c_im.astype(f32)

    lag_re, lag_im = [bb_re], [bb_im]
    for _ in range(S5_LAGS - 1):
        nre, nim = _cmul(ab_re[..., None], ab_im[..., None], lag_re[-1], lag_im[-1])
        lag_re.append(nre)
        lag_im.append(nim)
    lag = jnp.stack([jnp.stack(lag_re, axis=1), jnp.stack(lag_im, axis=1)], axis=2)
    lag = lag.reshape(S5_SLABS, S5_HALVES, hg, S5_LAGS, 2, p, h)
    bc = jnp.transpose(lag, (0, 1, 3, 2, 6, 4, 5)).reshape(S5_SLABS, S5_HALVES, S5_LAGS * hg * h, 2 * p).astype(bf16)

    a2_re, a2_im = _cmul(ab_re, ab_im, ab_re, ab_im)
    a4_re, a4_im = _cmul(a2_re, a2_im, a2_re, a2_im)
    bshape = (S5_SLABS // 2, SUBLANES, S5_PAIR_COLS // 2)
    a4r = jnp.broadcast_to(a4_re.reshape(S5_SLABS // 2, 1, S5_PAIR_COLS // 2), bshape)
    a4i = jnp.broadcast_to(a4_im.reshape(S5_SLABS // 2, 1, S5_PAIR_COLS // 2), bshape)

    ca_re, ca_im = _cmul(cr, ci, ab_re[:, None, :], ab_im[:, None, :])
    cmap = jnp.stack([jnp.stack([cr, ca_re], axis=1), jnp.stack([-ci, -ca_im], axis=1)], axis=1)
    cmap = cmap.reshape(S5_SLABS, S5_HALVES, hg, 2, 2, h, p)
    cc = jnp.transpose(cmap, (0, 1, 3, 2, 6, 4, 5)).reshape(S5_SLABS, 2 * S5_SLAB_STATES, 2 * h)
    cc = jnp.pad(cc, ((0, 0), (0, 0), (0, LANES - 2 * h))).astype(bf16)

    k0 = jnp.einsum('ghp,gpi->gih', cr, bb_re) - jnp.einsum('ghp,gpi->gih', ci, bb_im)
    kc = jnp.pad(k0.reshape(S5_SLABS // 2, 2 * LANES, h), ((0, 0), (0, 0), (0, LANES - h))).astype(bf16)
    return bc, cc, kc, a4r, a4i


def _tilers():
    p, h, sg = S5_STATE, S5_GROUP_DIM, S5_SLAB_GROUPS
    tw = np.zeros((2, p, 2, S5_HALF_GROUPS, p), np.float32)
    tw[np.arange(2)[:, None], np.arange(p)[None, :], np.arange(2)[:, None], :, np.arange(p)[None, :]] = 1.0
    tc = np.zeros((LANES, 2, sg, h), np.float32)
    for v in range(2):
        tc[v * h + np.arange(h), v, :, np.arange(h)] = 1.0
    tk = np.zeros((LANES, 2 * sg, h), np.float32)
    tk[np.arange(h), :, np.arange(h)] = 1.0
    as_bf16 = lambda t, cols: jnp.asarray(t.reshape(-1, cols), jnp.bfloat16)
    return as_bf16(tw, 2 * S5_HALF_STATES), as_bf16(tc, 2 * LANES), as_bf16(tk, 2 * LANES)


def _mixer(x, g1, lng, lnb, ws, bs, bc, cc, kc, a4r, a4i, d, bglu, own_w, ffn_w, layer):
    f32 = jnp.float32
    bf16 = jnp.bfloat16
    weights = (g1, lng, lnb, ws, bs, bc, cc, kc) + _tilers() + (a4r, a4i, d, bglu)
    n_tiles = SEQ // TT
    tile = lambda i: jnp.maximum(i - PREP_STEPS, 0)
    prep = lambda i: jnp.minimum(i, PREP_STEPS - 1)
    x_spec = pl.BlockSpec((BATCH, TT, D_MODEL), lambda i: (0, tile(i), 0))
    own_specs = [pl.BlockSpec((None, w.shape[1] // PREP_STEPS, w.shape[2]), lambda i: (layer, prep(i), 0))
                 for w in own_w]
    own_scratch = [pltpu.VMEM(w.shape[1:], bf16) for w in own_w]
    assert all(w.shape[1] % (PREP_STEPS * BF16_ROWS) == 0 for w in own_w)
    gu_rows = D_MODEL // n_tiles
    wd_rows = BF16_ROWS * (FFN_HIDDEN // (2 * LANES))
    wd_steps = n_tiles // (FFN_HIDDEN // wd_rows)
    assert gu_rows % BF16_ROWS == 0 and FFN_HIDDEN % wd_rows == 0 and wd_steps * (FFN_HIDDEN // wd_rows) == n_tiles
    gu_in = pl.BlockSpec((None, gu_rows, FFN_HIDDEN), lambda i: (layer, tile(i), 0))
    wd_in = pl.BlockSpec((None, wd_rows, D_MODEL), lambda i: (layer, tile(i) // wd_steps, 0))
    gu_out = pl.BlockSpec((gu_rows, FFN_HIDDEN), lambda i: (tile(i), 0))
    wd_out = pl.BlockSpec((wd_rows, D_MODEL), lambda i: (tile(i) // wd_steps, 0))
    gu_shape = jax.ShapeDtypeStruct((D_MODEL, FFN_HIDDEN), bf16)
    wd_shape = jax.ShapeDtypeStruct((FFN_HIDDEN, D_MODEL), bf16)
    return pl.pallas_call(
        _mixer_kernel,
        grid=(PREP_STEPS + n_tiles,),
        in_specs=[x_spec] + [_resident(w.shape) for w in weights] + own_specs + [gu_in, gu_in, wd_in],
        out_specs=[x_spec, gu_out, gu_out, wd_out],
        out_shape=[jax.ShapeDtypeStruct((BATCH, SEQ, D_MODEL), f32), gu_shape, gu_shape, wd_shape],
        scratch_shapes=own_scratch + [
            pltpu.VMEM((S5_SLABS, S5_HALVES, 2 * LANES, 2 * S5_HALF_STATES), bf16),
            pltpu.VMEM((S5_SLABS, 2 * S5_SLAB_STATES, 2 * LANES), bf16),
            pltpu.VMEM((S5_SLABS // 2, 2 * LANES, 2 * LANES), bf16),
            pltpu.VMEM((TM, D_MODEL), bf16),
            pltpu.VMEM((TM, GMLP_WIDTH), f32),
            pltpu.VMEM((TM, GMLP_WIDTH), f32),
            pltpu.VMEM((TM, GMLP_WIDTH), bf16),
            pltpu.VMEM((TM, GMLP_WIDTH), bf16),
            pltpu.VMEM((S5_SLABS, TM, LANES), f32),
            pltpu.VMEM((S5_SLABS, S5_PARITIES, HM + SUBLANES, LANES), f32),
            pltpu.VMEM((S5_SLABS // 2, HM, S5_PAIR_COLS), f32),
            pltpu.VMEM((S5_SLABS // 2, HM, S5_PAIR_COLS), bf16),
            pltpu.VMEM((S5_SLABS // 2, SUBLANES, S5_PAIR_COLS), f32),
            pltpu.VMEM((TM, S5_WIDTH), f32),
            pltpu.VMEM((S5_SLABS, TM, LANES), f32),
            pltpu.VMEM((S5_SLABS, TM, LANES), f32),
            pltpu.VMEM((TM, S5_WIDTH), bf16),
            pltpu.VMEM((TM, D_MODEL), f32),
            pltpu.VMEM((TM, D_MODEL), bf16),
        ],
        compiler_params=pltpu.CompilerParams(
            dimension_semantics=("arbitrary",), vmem_limit_bytes=VMEM_LIMIT),
        name="mixer",
    )(x, *weights, *own_w, *ffn_w)


def _ffn(x, g2, wg, wu, wd, gf, final_norm):
    f32 = jnp.float32
    n = x.shape[0]
    weights = (g2, wg, wu, wd, gf)
    x_spec = pl.BlockSpec((FFN_TM, D_MODEL), lambda i: (i, 0))
    return pl.pallas_call(
        functools.partial(_ffn_kernel, final_norm=final_norm),
        grid=(n // FFN_TM,),
        in_specs=[x_spec] + [_resident(w.shape) for w in weights],
        out_specs=x_spec,
        out_shape=jax.ShapeDtypeStruct((n, D_MODEL), f32),
        scratch_shapes=[
            pltpu.VMEM((FFN_TM, D_MODEL), jnp.bfloat16),
            pltpu.VMEM((FFN_TM, FFN_HIDDEN), jnp.bfloat16),
        ],
        compiler_params=pltpu.CompilerParams(
            dimension_semantics=("arbitrary",), vmem_limit_bytes=VMEM_LIMIT),
        name="ffn",
    )(x, *weights)


def kernel(x, norm1_g, w_in, gmlp_ln_g, gmlp_ln_b, gmlp_ws, gmlp_bs, s5_lambda_re, s5_lambda_im, s5_log_dt, s5_b_re, s5_b_im, s5_c_re, s5_c_im, s5_d, s5_w_glu, s5_b_glu, w_branch_a, w_branch_b, w_out, norm2_g, w_ffn_gate, w_ffn_up, w_ffn_down, norm_f_g):
    f32 = jnp.float32
    bf16 = jnp.bfloat16
    depth = norm1_g.shape[0]
    cidx = jnp.arange(GMLP_BLOCK) // CHUNK
    mask = cidx[None, :] <= cidx[:, None]
    row = lambda v: v.astype(f32).reshape(1, -1)
    for l in range(depth):
        ws = jnp.where(mask[None], gmlp_ws[l], jnp.zeros_like(gmlp_ws[l])).astype(bf16)
        bs = jnp.broadcast_to(gmlp_bs[l].astype(f32)[:, :, None], (GMLP_GROUPS, GMLP_BLOCK, LANES))
        bc, cc, kc, a4r, a4i = _s5_params(s5_lambda_re[l], s5_lambda_im[l], s5_log_dt[l],
                                          s5_b_re[l], s5_b_im[l], s5_c_re[l], s5_c_im[l])
        own_w = (w_in, s5_w_glu, w_branch_a, w_branch_b, w_out)
        ffn_w = (w_ffn_gate, w_ffn_up, w_ffn_down)
        x, wg, wu, wd = _mixer(x, row(norm1_g[l]), row(gmlp_ln_g[l]), row(gmlp_ln_b[l]), ws, bs,
                               bc, cc, kc, a4r, a4i, row(s5_d[l]), row(0.5 * s5_b_glu[l]), own_w, ffn_w, l)
        y = _ffn(x.reshape(BATCH * SEQ, D_MODEL), row(norm2_g[l]), wg, wu, wd, row(norm_f_g), l == depth - 1)
        x = y.reshape(BATCH, SEQ, D_MODEL)
    return x
```

```python
import functools

import jax
import jax.numpy as jnp
import numpy as np
from jax import lax
from jax.experimental import pallas as pl
from jax.experimental.pallas import tpu as pltpu

D_MODEL = 1024
BATCH = 4
SEQ = 8192
CHUNK = 64
GMLP_BLOCK = 128
GMLP_GROUP_DIM = 128
GMLP_WIDTH = D_MODEL
GMLP_GROUPS = GMLP_WIDTH // GMLP_GROUP_DIM
S5_GROUP_DIM = 16
S5_WIDTH = D_MODEL // 2
S5_STATE = 64
FFN_HIDDEN = -(-8 * D_MODEL // 768) * 256
EPS = 1e-6

LANES = 128
SUBLANES = 8
BF16_ROWS = 2 * SUBLANES
S5_SLABS = S5_WIDTH // LANES
S5_SLAB_GROUPS = LANES // S5_GROUP_DIM
S5_SLAB_STATES = S5_SLAB_GROUPS * S5_STATE
S5_PAIR_COLS = 2 * 2 * S5_SLAB_STATES
S5_LAGS = 4
S5_HALVES = 2
S5_HALF_GROUPS = S5_SLAB_GROUPS // S5_HALVES
S5_HALF_STATES = S5_HALF_GROUPS * S5_STATE
S5_PARITIES = 2
TT = GMLP_BLOCK
TM = BATCH * TT
HM = TM // S5_PARITIES
ROWS_PER_STEP = 2 * SUBLANES
FFN_TM = 1024
FFN_SUBTILES = 4
PREP_STEPS = 8
VMEM_LIMIT = 60 * 1024 * 1024

_GELU_C = float(np.sqrt(2.0 / np.pi))
_LOG2_GROUP_DIM = S5_GROUP_DIM.bit_length() - 1
_LOG2_STATE = S5_STATE.bit_length() - 1
assert 1 << _LOG2_GROUP_DIM == S5_GROUP_DIM and 1 << _LOG2_STATE == S5_STATE


def _gelu(x):
    hx = 0.5 * x
    return hx * jnp.tanh(x * ((x * x) * (_GELU_C * 0.044715) + _GELU_C)) + hx


def _twice_sigmoid_of_twice(x_half):
    return jnp.tanh(x_half) + 1.0


def _mixer_kernel(x_ref, g1_ref, lng_ref, lnb_ref, ws_ref, bs_ref,
                  bc_ref, cc_ref, kc_ref, tw_ref, tc_ref, tk_ref, a4r_ref, a4i_ref, d_ref, bglu_ref,
                  win_f, wglu_f, pa_f, pb_f, wout_f, wg_ref, wu_ref, wd_ref,
                  o_ref, wg_o, wu_o, wd_o,
                  win_ref, wglu_ref, pa_ref, pb_ref, wout_ref, w4_s, cw_s, k0_s, h_s, ug_s, vg_s, v_s, ya_s, xbs_s, uext_s, st_s, xs_s, hst_s, ypre_s, yslab_s, ybm_s, yb_s,
                  sga_s, mg_s):
    f32 = jnp.float32
    bf16 = jnp.bfloat16
    step = pl.program_id(0)

    def block_diag(compact, tiler, row_group, col_group):
        full = jnp.dot(compact, tiler, preferred_element_type=f32)
        rg = row_group(lax.broadcasted_iota(jnp.int32, full.shape, 0))
        cg = col_group(lax.broadcasted_iota(jnp.int32, full.shape, 1))
        return jnp.where(rg == cg, full, 0.0).astype(bf16)

    s1 = GMLP_WIDTH
    s2 = 2 * GMLP_WIDTH
    s3 = s2 + S5_WIDTH
    s4 = s3 + D_MODEL

    @pl.when(step < PREP_STEPS)
    def _():
        def rows(ref):
            n = ref.shape[0]
            return pl.ds(pl.multiple_of(step * n, n), n)
        gate_col = lax.broadcasted_iota(jnp.int32, win_f.shape, 1) >= s3
        win_ref[rows(win_f), :] = (win_f[...] * jnp.where(gate_col, 0.5, 1.0)).astype(bf16)
        wglu_ref[rows(wglu_f), :] = (0.5 * wglu_f[...]).astype(bf16)
        pa_ref[rows(pa_f), :] = pa_f[...].astype(bf16)
        pb_ref[rows(pb_f), :] = (0.5 * pb_f[...]).astype(bf16)
        wout_ref[rows(wout_f), :] = (0.5 * wout_f[...]).astype(bf16)

    @pl.when(step == PREP_STEPS)
    def _():
        hst_s[...] = jnp.zeros_like(hst_s)
        uext_s[:, :, 0:SUBLANES, :] = jnp.zeros((S5_SLABS, S5_PARITIES, SUBLANES, LANES), f32)
        sg, hg = S5_SLAB_GROUPS, S5_HALF_GROUPS
        lg_half = S5_HALF_STATES.bit_length() - 1
        for q in range(S5_SLABS):
            for hf in range(S5_HALVES):
                w4_s[q, hf] = block_diag(bc_ref[q, hf], tw_ref[...],
                                         lambda r: (r >> _LOG2_GROUP_DIM) & (hg - 1),
                                         lambda c: (c >> _LOG2_STATE) & (hg - 1))
            cw_s[q] = block_diag(cc_ref[q], tc_ref[...],
                                 lambda r: ((r >> (lg_half + 1)) * hg) + ((r >> _LOG2_STATE) & (hg - 1)),
                                 lambda c: (c >> _LOG2_GROUP_DIM) & (sg - 1))
        for m in range(S5_SLABS // 2):
            k0_s[m] = block_diag(kc_ref[m], tk_ref[...],
                                 lambda r: r >> _LOG2_GROUP_DIM, lambda c: c >> _LOG2_GROUP_DIM)

    cn = 256
    ns = S5_SLAB_STATES
    hs = S5_HALF_STATES
    n_parts = 4
    steps_per_part = HM // ROWS_PER_STEP // n_parts
    frames = TT // S5_PARITIES

    hb_n = BATCH // 2
    hrows = hb_n * TT

    def rows(hb):
        return slice(hb * hrows, (hb + 1) * hrows)

    def rmsnorm(hb):
        x = x_ref[hb * hb_n:(hb + 1) * hb_n].reshape(hrows, D_MODEL)
        ms = jnp.mean(x * x, axis=-1, keepdims=True)
        h_s[rows(hb), :] = (x * lax.rsqrt(ms + EPS) * g1_ref[...]).astype(bf16)

    def u_chunk(c, hb):
        ug_s[rows(hb), c:c + cn] = _gelu(jnp.dot(h_s[rows(hb), :], win_ref[:, c:c + cn], preferred_element_type=f32))

    def v_chunk(c, hb):
        vg_s[rows(hb), c:c + cn] = _gelu(
            jnp.dot(h_s[rows(hb), :], win_ref[:, s1 + c:s1 + c + cn], preferred_element_type=f32))

    def layer_norm(hb):
        vg = vg_s[rows(hb), :]
        mu = jnp.mean(vg, axis=-1, keepdims=True)
        vc = vg - mu
        var = jnp.mean(vc * vc, axis=-1, keepdims=True)
        v_s[rows(hb), :] = (vc * lax.rsqrt(var + EPS) * lng_ref[...] + lnb_ref[...]).astype(bf16)

    def xb_proj():
        xb = jnp.dot(h_s[...], win_ref[:, s2:s3], preferred_element_type=f32)
        for q in range(S5_SLABS):
            xbs_s[q] = xb[:, q * LANES:(q + 1) * LANES]
        for q in range(S5_SLABS):
            for r in range(S5_PARITIES):
                for b in range(BATCH):
                    uext_s[q, r, pl.ds(SUBLANES + b, frames, stride=BATCH), :] = (
                        xbs_s[q, pl.ds(b * TT + r, frames, stride=S5_PARITIES), :])

    def bu(q):
        m, ql = divmod(q, 2)
        lags = []
        for d in range(S5_LAGS):
            off = SUBLANES - BATCH * ((d + 1) // 2)
            lags.append(uext_s[q, d % 2, off:off + HM, :])
        low = lax.broadcasted_iota(jnp.int32, (HM, LANES), 1) < LANES // 2
        for hf in range(S5_HALVES):
            tiles = []
            for d in range(0, S5_LAGS, 2):
                if hf == 0:
                    t = jnp.where(low, lags[d], pltpu.roll(lags[d + 1], LANES // 2, 1))
                else:
                    t = jnp.where(low, pltpu.roll(lags[d], LANES // 2, 1), lags[d + 1])
                tiles.append(t.astype(bf16))
            lhs = jnp.concatenate(tiles, axis=1)
            c0 = ql * 2 * ns + hf * 2 * hs
            st_s[m, :, c0:c0 + 2 * hs] = jnp.dot(lhs, w4_s[q, hf], preferred_element_type=f32)

    def split(z):
        res = [z[:, k * 2 * hs:k * 2 * hs + hs] for k in range(2 * S5_HALVES)]
        ims = [z[:, k * 2 * hs + hs:(k + 1) * 2 * hs] for k in range(2 * S5_HALVES)]
        return jnp.concatenate(res, axis=1), jnp.concatenate(ims, axis=1)

    def merge(zr, zi):
        parts = []
        for k in range(2 * S5_HALVES):
            parts += [zr[:, k * hs:(k + 1) * hs], zi[:, k * hs:(k + 1) * hs]]
        return jnp.concatenate(parts, axis=1)

    carry = {}

    def scan_part(m, p):
        xr, xi = split(hst_s[m]) if p == 0 else carry[m]
        for i in range(p * steps_per_part, (p + 1) * steps_per_part):
            r0 = i * ROWS_PER_STEP
            outs = []
            for k in range(ROWS_PER_STEP // SUBLANES):
                br, bi = split(st_s[m, r0 + k * SUBLANES:r0 + (k + 1) * SUBLANES, :])
                a4r = a4r_ref[m]
                a4i = a4i_ref[m]
                xr, xi = a4r * xr - a4i * xi + br, a4r * xi + a4i * xr + bi
                outs.append(merge(xr, xi))
            xs_s[m, r0:r0 + ROWS_PER_STEP, :] = jnp.concatenate(outs, axis=0).astype(bf16)
        carry[m] = (xr, xi)
        if p == n_parts - 1:
            hst_s[m] = merge(xr, xi)

    def cur(q, r):
        return uext_s[q, r, SUBLANES:SUBLANES + HM, :]

    def cy(m):
        u_odd = jnp.concatenate([cur(2 * m, 1), cur(2 * m + 1, 1)], axis=1)
        direct = jnp.dot(u_odd.astype(bf16), k0_s[m], preferred_element_type=f32)
        for ql in range(2):
            q = 2 * m + ql
            cs = slice(q * LANES, (q + 1) * LANES)
            y2 = jnp.dot(xs_s[m, :, ql * 2 * ns:(ql + 1) * 2 * ns], cw_s[q], preferred_element_type=f32)
            ypre_s[0:HM, cs] = y2[:, 0:LANES] + d_ref[:, cs] * cur(q, 0)
            ypre_s[HM:TM, cs] = (y2[:, LANES:2 * LANES] + direct[:, ql * LANES:(ql + 1) * LANES]
                                 + d_ref[:, cs] * cur(q, 1))

    def ga_chunk(c, hb):
        sga_s[rows(hb), c:c + cn] = _twice_sigmoid_of_twice(
            jnp.dot(h_s[rows(hb), :], win_ref[:, s3 + c:s3 + c + cn], preferred_element_type=f32))

    def glu():
        yact = _gelu(ypre_s[...])
        z = jnp.dot(yact.astype(bf16), wglu_ref[...], preferred_element_type=f32) + bglu_ref[...]
        ybp = yact * _twice_sigmoid_of_twice(z)
        for q in range(S5_SLABS):
            yslab_s[q] = ybp[:, q * LANES:(q + 1) * LANES]
        for q in range(S5_SLABS):
            for r in range(S5_PARITIES):
                for b in range(BATCH):
                    ybm_s[q, pl.ds(b * TT + r, frames, stride=S5_PARITIES), :] = (
                        yslab_s[q, pl.ds(r * HM + b, frames, stride=BATCH), :])
        for q in range(S5_SLABS):
            yb_s[:, q * LANES:(q + 1) * LANES] = ybm_s[q].astype(bf16)

    def gmlp(g, hb):
        cs = slice(g * GMLP_GROUP_DIM, (g + 1) * GMLP_GROUP_DIM)
        bb = range(hb * hb_n, (hb + 1) * hb_n)
        vcat = jnp.concatenate([v_s[b * TT:(b + 1) * TT, cs] for b in bb], axis=1)
        mixed = jnp.dot(ws_ref[g], vcat, preferred_element_type=f32)
        for k, b in enumerate(bb):
            rs = slice(b * TT, (b + 1) * TT)
            ya_s[rs, cs] = (ug_s[rs, cs] * (mixed[:, k * LANES:(k + 1) * LANES] + bs_ref[g])).astype(bf16)

    def merge_chunk(c, cn, hb):
        rs = rows(hb)
        gb = jnp.dot(h_s[rs, :], win_ref[:, s4 + c:s4 + c + cn], preferred_element_type=f32)
        ma = jnp.dot(ya_s[rs, :], pa_ref[:, c:c + cn], preferred_element_type=f32)
        mb = jnp.dot(yb_s[rs, :], pb_ref[:, c:c + cn], preferred_element_type=f32)
        mg_s[rs, c:c + cn] = (sga_s[rs, c:c + cn] * ma + _twice_sigmoid_of_twice(gb) * mb).astype(bf16)

    def out_chunk(c, cn, hb):
        res = x_ref[hb * hb_n:(hb + 1) * hb_n, :, c:c + cn].reshape(hrows, cn) + jnp.dot(
            mg_s[rows(hb), :], wout_ref[:, c:c + cn], preferred_element_type=f32)
        o_ref[hb * hb_n:(hb + 1) * hb_n, :, c:c + cn] = res.reshape(hb_n, TT, cn)

    @pl.when(step >= PREP_STEPS)
    def _():
        rmsnorm(0)
        for c in range(0, GMLP_WIDTH, cn):
            u_chunk(c, 0)
        for c in range(0, GMLP_WIDTH, cn):
            v_chunk(c, 0)
        rmsnorm(1)
        xb_proj()
        layer_norm(0)
        for c in range(0, D_MODEL, cn):
            ga_chunk(c, 0)
        for q in range(S5_SLABS):
            bu(q)
        for c in range(0, GMLP_WIDTH, cn):
            u_chunk(c, 1)
        for p in range(n_parts):
            scan_part(0, p)
        cy(0)
        for c in range(0, GMLP_WIDTH, cn):
            v_chunk(c, 1)
        for p in range(n_parts):
            scan_part(1, p)
        layer_norm(1)
        cy(1)
        for c in range(0, D_MODEL, cn):
            ga_chunk(c, 1)
        for q in range(S5_SLABS):
            for r in range(S5_PARITIES):
                uext_s[q, r, 0:SUBLANES, :] = uext_s[q, r, HM:HM + SUBLANES, :]
        glu()
        for hb in range(2):
            for g in range(GMLP_GROUPS):
                gmlp(g, hb)
            for c in range(0, D_MODEL, 2 * cn):
                merge_chunk(c, 2 * cn, hb)
            for c in range(0, D_MODEL, 2 * cn):
                out_chunk(c, 2 * cn, hb)

        wg_o[...] = (0.5 * wg_ref[...]).astype(bf16)
        wu_o[...] = wu_ref[...].astype(bf16)
        wd_o[...] = wd_ref[...].astype(bf16)


def _ffn_kernel(x_ref, g2_ref, wg_ref, wu_ref, wd_ref, gf_ref, o_ref, h_s, hid_s, *, final_norm):
    bf16 = jnp.bfloat16
    cn = 256
    hr = FFN_TM // FFN_SUBTILES
    for r in range(0, FFN_TM, hr):
        rs = slice(r, r + hr)
        x = x_ref[rs, :]
        ms = jnp.mean(x * x, axis=-1, keepdims=True)
        h_s[rs, :] = (x * lax.rsqrt(ms + EPS) * g2_ref[...]).astype(bf16)
        y = x_ref[rs, :]
        for c in range(0, FFN_HIDDEN, cn):
            gate = jnp.dot(h_s[rs, :], wg_ref[:, c:c + cn], preferred_element_type=jnp.float32)
            up = jnp.dot(h_s[rs, :], wu_ref[:, c:c + cn], preferred_element_type=jnp.float32)
            hid = (gate * _twice_sigmoid_of_twice(gate) * up).astype(bf16)
            y = y + jnp.dot(hid, wd_ref[c:c + cn, :], preferred_element_type=jnp.float32)
        if final_norm:
            ms2 = jnp.mean(y * y, axis=-1, keepdims=True)
            y = y * lax.rsqrt(ms2 + EPS) * gf_ref[...]
        o_ref[rs, :] = y


def _resident(shape):
    nd = len(shape)
    return pl.BlockSpec(shape, lambda *_: (0,) * nd, pipeline_mode=pl.Buffered(1))


def _cmul(ar, ai, br, bi):
    return ar * br - ai * bi, ar * bi + ai * br


def _s5_params(lam_re, lam_im, log_dt, b_re, b_im, c_re, c_im):
    f32 = jnp.float32
    bf16 = jnp.bfloat16
    p, h = S5_STATE, S5_GROUP_DIM
    hg = S5_HALF_GROUPS
    dt = jnp.exp(log_dt.astype(f32))[:, None]
    lr = lam_re.astype(f32)
    li = lam_im.astype(f32)
    mag = jnp.exp(lr * dt)
    ab_re = mag * jnp.cos(li * dt)
    ab_im = mag * jnp.sin(li * dt)
    den = lr * lr + li * li
    nr = ab_re - 1.0
    coef_re = (nr * lr + ab_im * li) / den
    coef_im = (ab_im * lr - nr * li) / den
    br = b_re.astype(f32)
    bi = b_im.astype(f32)
    bb_re = coef_re[..., None] * br - coef_im[..., None] * bi
    bb_im = coef_re[..., None] * bi + coef_im[..., None] * br
    cr = c_re.astype(f32)
    ci = c_im.astype(f32)

    lag_re, lag_im = [bb_re], [bb_im]
    for _ in range(S5_LAGS - 1):
        nre, nim = _cmul(ab_re[..., None], ab_im[..., None], lag_re[-1], lag_im[-1])
        lag_re.append(nre)
        lag_im.append(nim)
    lag = jnp.stack([jnp.stack(lag_re, axis=1), jnp.stack(lag_im, axis=1)], axis=2)
    lag = lag.reshape(S5_SLABS, S5_HALVES, hg, S5_LAGS, 2, p, h)
    bc = jnp.transpose(lag, (0, 1, 3, 2, 6, 4, 5)).reshape(S5_SLABS, S5_HALVES, S5_LAGS * hg * h, 2 * p).astype(bf16)

    a2_re, a2_im = _cmul(ab_re, ab_im, ab_re, ab_im)
    a4_re, a4_im = _cmul(a2_re, a2_im, a2_re, a2_im)
    bshape = (S5_SLABS // 2, SUBLANES, S5_PAIR_COLS // 2)
    a4r = jnp.broadcast_to(a4_re.reshape(S5_SLABS // 2, 1, S5_PAIR_COLS // 2), bshape)
    a4i = jnp.broadcast_to(a4_im.reshape(S5_SLABS // 2, 1, S5_PAIR_COLS // 2), bshape)

    ca_re, ca_im = _cmul(cr, ci, ab_re[:, None, :], ab_im[:, None, :])
    cmap = jnp.stack([jnp.stack([cr, ca_re], axis=1), jnp.stack([-ci, -ca_im], axis=1)], axis=1)
    cmap = cmap.reshape(S5_SLABS, S5_HALVES, hg, 2, 2, h, p)
    cc = jnp.transpose(cmap, (0, 1, 3, 2, 6, 4, 5)).reshape(S5_SLABS, 2 * S5_SLAB_STATES, 2 * h)
    cc = jnp.pad(cc, ((0, 0), (0, 0), (0, LANES - 2 * h))).astype(bf16)

    k0 = jnp.einsum('ghp,gpi->gih', cr, bb_re) - jnp.einsum('ghp,gpi->gih', ci, bb_im)
    kc = jnp.pad(k0.reshape(S5_SLABS // 2, 2 * LANES, h), ((0, 0), (0, 0), (0, LANES - h))).astype(bf16)
    return bc, cc, kc, a4r, a4i


def _tilers():
    p, h, sg = S5_STATE, S5_GROUP_DIM, S5_SLAB_GROUPS
    tw = np.zeros((2, p, 2, S5_HALF_GROUPS, p), np.float32)
    tw[np.arange(2)[:, None], np.arange(p)[None, :], np.arange(2)[:, None], :, np.arange(p)[None, :]] = 1.0
    tc = np.zeros((LANES, 2, sg, h), np.float32)
    for v in range(2):
        tc[v * h + np.arange(h), v, :, np.arange(h)] = 1.0
    tk = np.zeros((LANES, 2 * sg, h), np.float32)
    tk[np.arange(h), :, np.arange(h)] = 1.0
    as_bf16 = lambda t, cols: jnp.asarray(t.reshape(-1, cols), jnp.bfloat16)
    return as_bf16(tw, 2 * S5_HALF_STATES), as_bf16(tc, 2 * LANES), as_bf16(tk, 2 * LANES)


def _mixer(x, g1, lng, lnb, ws, bs, bc, cc, kc, a4r, a4i, d, bglu, own_w, ffn_w, layer):
    f32 = jnp.float32
    bf16 = jnp.bfloat16
    weights = (g1, lng, lnb, ws, bs, bc, cc, kc) + _tilers() + (a4r, a4i, d, bglu)
    n_tiles = SEQ // TT
    tile = lambda i: jnp.maximum(i - PREP_STEPS, 0)
    prep = lambda i: jnp.minimum(i, PREP_STEPS - 1)
    x_spec = pl.BlockSpec((BATCH, TT, D_MODEL), lambda i: (0, tile(i), 0))
    own_specs = [pl.BlockSpec((None, w.shape[1] // PREP_STEPS, w.shape[2]), lambda i: (layer, prep(i), 0))
                 for w in own_w]
    own_scratch = [pltpu.VMEM(w.shape[1:], bf16) for w in own_w]
    assert all(w.shape[1] % (PREP_STEPS * BF16_ROWS) == 0 for w in own_w)
    gu_rows = D_MODEL // n_tiles
    wd_rows = BF16_ROWS * (FFN_HIDDEN // (2 * LANES))
    wd_steps = n_tiles // (FFN_HIDDEN // wd_rows)
    assert gu_rows % BF16_ROWS == 0 and FFN_HIDDEN % wd_rows == 0 and wd_steps * (FFN_HIDDEN // wd_rows) == n_tiles
    gu_in = pl.BlockSpec((None, gu_rows, FFN_HIDDEN), lambda i: (layer, tile(i), 0))
    wd_in = pl.BlockSpec((None, wd_rows, D_MODEL), lambda i: (layer, tile(i) // wd_steps, 0))
    gu_out = pl.BlockSpec((gu_rows, FFN_HIDDEN), lambda i: (tile(i), 0))
    wd_out = pl.BlockSpec((wd_rows, D_MODEL), lambda i: (tile(i) // wd_steps, 0))
    gu_shape = jax.ShapeDtypeStruct((D_MODEL, FFN_HIDDEN), bf16)
    wd_shape = jax.ShapeDtypeStruct((FFN_HIDDEN, D_MODEL), bf16)
    return pl.pallas_call(
        _mixer_kernel,
        grid=(PREP_STEPS + n_tiles,),
        in_specs=[x_spec] + [_resident(w.shape) for w in weights] + own_specs + [gu_in, gu_in, wd_in],
        out_specs=[x_spec, gu_out, gu_out, wd_out],
        out_shape=[jax.ShapeDtypeStruct((BATCH, SEQ, D_MODEL), f32), gu_shape, gu_shape, wd_shape],
        scratch_shapes=own_scratch + [
            pltpu.VMEM((S5_SLABS, S5_HALVES, 2 * LANES, 2 * S5_HALF_STATES), bf16),
            pltpu.VMEM((S5_SLABS, 2 * S5_SLAB_STATES, 2 * LANES), bf16),
            pltpu.VMEM((S5_SLABS // 2, 2 * LANES, 2 * LANES), bf16),
            pltpu.VMEM((TM, D_MODEL), bf16),
            pltpu.VMEM((TM, GMLP_WIDTH), f32),
            pltpu.VMEM((TM, GMLP_WIDTH), f32),
            pltpu.VMEM((TM, GMLP_WIDTH), bf16),
            pltpu.VMEM((TM, GMLP_WIDTH), bf16),
            pltpu.VMEM((S5_SLABS, TM, LANES), f32),
            pltpu.VMEM((S5_SLABS, S5_PARITIES, HM + SUBLANES, LANES), f32),
            pltpu.VMEM((S5_SLABS // 2, HM, S5_PAIR_COLS), f32),
            pltpu.VMEM((S5_SLABS // 2, HM, S5_PAIR_COLS), bf16),
            pltpu.VMEM((S5_SLABS // 2, SUBLANES, S5_PAIR_COLS), f32),
            pltpu.VMEM((TM, S5_WIDTH), f32),
            pltpu.VMEM((S5_SLABS, TM, LANES), f32),
            pltpu.VMEM((S5_SLABS, TM, LANES), f32),
            pltpu.VMEM((TM, S5_WIDTH), bf16),
            pltpu.VMEM((TM, D_MODEL), f32),
            pltpu.VMEM((TM, D_MODEL), bf16),
        ],
        compiler_params=pltpu.CompilerParams(
            dimension_semantics=("arbitrary",), vmem_limit_bytes=VMEM_LIMIT),
        name="mixer",
    )(x, *weights, *own_w, *ffn_w)


def _ffn(x, g2, wg, wu, wd, gf, final_norm):
    f32 = jnp.float32
    n = x.shape[0]
    weights = (g2, wg, wu, wd, gf)
    x_spec = pl.BlockSpec((FFN_TM, D_MODEL), lambda i: (i, 0))
    return pl.pallas_call(
        functools.partial(_ffn_kernel, final_norm=final_norm),
        grid=(n // FFN_TM,),
        in_specs=[x_spec] + [_resident(w.shape) for w in weights],
        out_specs=x_spec,
        out_shape=jax.ShapeDtypeStruct((n, D_MODEL), f32),
        scratch_shapes=[
            pltpu.VMEM((FFN_TM, D_MODEL), jnp.bfloat16),
            pltpu.VMEM((FFN_TM, FFN_HIDDEN), jnp.bfloat16),
        ],
        compiler_params=pltpu.CompilerParams(
            dimension_semantics=("arbitrary",), vmem_limit_bytes=VMEM_LIMIT),
        name="ffn",
    )(x, *weights)


def kernel(x, norm1_g, w_in, gmlp_ln_g, gmlp_ln_b, gmlp_ws, gmlp_bs, s5_lambda_re, s5_lambda_im, s5_log_dt, s5_b_re, s5_b_im, s5_c_re, s5_c_im, s5_d, s5_w_glu, s5_b_glu, w_branch_a, w_branch_b, w_out, norm2_g, w_ffn_gate, w_ffn_up, w_ffn_down, norm_f_g):
    f32 = jnp.float32
    bf16 = jnp.bfloat16
    depth = norm1_g.shape[0]
    cidx = jnp.arange(GMLP_BLOCK) // CHUNK
    mask = cidx[None, :] <= cidx[:, None]
    row = lambda v: v.astype(f32).reshape(1, -1)
    for l in range(depth):
        ws = jnp.where(mask[None], gmlp_ws[l], jnp.zeros_like(gmlp_ws[l])).astype(bf16)
        bs = jnp.broadcast_to(gmlp_bs[l].astype(f32)[:, :, None], (GMLP_GROUPS, GMLP_BLOCK, LANES))
        bc, cc, kc, a4r, a4i = _s5_params(s5_lambda_re[l], s5_lambda_im[l], s5_log_dt[l],
                                          s5_b_re[l], s5_b_im[l], s5_c_re[l], s5_c_im[l])
        own_w = (w_in, s5_w_glu, w_branch_a, w_branch_b, w_out)
        ffn_w = (w_ffn_gate, w_ffn_up, w_ffn_down)
        x, wg, wu, wd = _mixer(x, row(norm1_g[l]), row(gmlp_ln_g[l]), row(gmlp_ln_b[l]), ws, bs,
                               bc, cc, kc, a4r, a4i, row(s5_d[l]), row(0.5 * s5_b_glu[l]), own_w, ffn_w, l)
        y = _ffn(x.reshape(BATCH * SEQ, D_MODEL), row(norm2_g[l]), wg, wu, wd, row(norm_f_g), l == depth - 1)
        x = y.reshape(BATCH, SEQ, D_MODEL)
    return x
```
